```python
import jax
import jax.numpy as jnp
from jax import lax
import numpy as np

D_MODEL = 1024
BATCH = 2
SEQ = 16384
DEPTH = 2

GRID_W = 64
CTX_LEN = 256
HEAD_DIM = 64
ROPE_BASE = 10000.0
NEG_INF = -1e30
EPS = 1e-6

POOL_WINDOWS = (2, 4, 8, 16)
POOL_GROUPS = 4
POOL_GROUP_DIM = 64
POOL_W = POOL_GROUPS * POOL_GROUP_DIM
NA_HEADS = 4
NA_ROWS = 8
NA_COLS = 16
NA_W = NA_HEADS * HEAD_DIM
SWA_HEADS = 4
SWA_KV_HEADS = 2
SWA_WINDOW = 128
SWA_BLOCK = 128
MLA_HEADS = 4
MLA_Q_RANK = 256
MLA_KV_RANK = 128
MLA_NOPE = 64
MLA_ROPE = 32
MLA_V = 64
MLA_Q_BLOCK = 128
MLA_SCALE = (MLA_NOPE + MLA_ROPE) ** -0.5
N_BRANCH = 4
BRANCH_W = 256
FFN_HIDDEN = -(-8 * D_MODEL // (3 * 256)) * 256

OFF_NA_K = 0
OFF_NA_V = OFF_NA_K + NA_W
OFF_SWA_K = OFF_NA_V + NA_W
OFF_SWA_V = OFF_SWA_K + SWA_KV_HEADS * HEAD_DIM
OFF_MLA_CKV = OFF_SWA_V + SWA_KV_HEADS * HEAD_DIM
OFF_MLA_KR = OFF_MLA_CKV + MLA_KV_RANK
KV_COLS = OFF_MLA_KR + MLA_ROPE
OFF_NA_Q = KV_COLS
OFF_SWA_Q = OFF_NA_Q + NA_W
OFF_MLA_CQ = OFF_SWA_Q + SWA_HEADS * HEAD_DIM
OFF_POOL = OFF_MLA_CQ + MLA_Q_RANK
OFF_GATE = OFF_POOL + POOL_W
IN_COLS = OFF_GATE + N_BRANCH * D_MODEL

kernel_name = "hybrid_pool_natten_swa_mla_prefix_dit"

F32 = jnp.float32


def rms_norm(x, g):
    xf = x.astype(F32)
    y = xf * lax.rsqrt(jnp.mean(xf * xf, axis=-1, keepdims=True) + EPS)
    return (y * g.astype(F32)).astype(x.dtype)


def modulation(cond, w, b, k):
    m = jax.nn.silu(cond) @ w[:, :k * D_MODEL] + b[:k * D_MODEL]
    return jnp.split(m, k, axis=-1)


def modulate(x, g, shift, scale):
    return rms_norm(x, g) * (1 + scale[..., None, :]) + shift[..., None, :]


def axial_rope(n, dim):
    t = jnp.arange(n, dtype=jnp.int32)
    row = (t // GRID_W).astype(F32)
    col = (t % GRID_W).astype(F32)
    n_freq = dim // 4
    inv = jnp.power(ROPE_BASE, -jnp.arange(n_freq, dtype=F32) / n_freq)
    ang = jnp.concatenate([row[:, None] * inv, col[:, None] * inv], axis=-1)
    return jnp.cos(ang), jnp.sin(ang)


def apply_rope(x, cos, sin):
    half = x.shape[-1] // 2
    xf = x.astype(F32)
    x1, x2 = xf[..., :half], xf[..., half:]
    return jnp.concatenate([x1 * cos - x2 * sin, x2 * cos + x1 * sin], axis=-1).astype(x.dtype)


def kv_heads(u, kv_norm_g, rope):
    B, n, _ = u.shape
    na_k = u[..., OFF_NA_K:OFF_NA_V].reshape(B, n, NA_HEADS, HEAD_DIM)
    na_v = u[..., OFF_NA_V:OFF_SWA_K].reshape(B, n, NA_HEADS, HEAD_DIM)
    swa_k = u[..., OFF_SWA_K:OFF_SWA_V].reshape(B, n, SWA_KV_HEADS, HEAD_DIM)
    swa_v = u[..., OFF_SWA_V:OFF_MLA_CKV].reshape(B, n, SWA_KV_HEADS, HEAD_DIM)
    ckv = rms_norm(u[..., OFF_MLA_CKV:OFF_MLA_KR], kv_norm_g)
    kr = u[..., OFF_MLA_KR:KV_COLS]
    if rope is not None:
        cos_h, sin_h, cos_r, sin_r = rope
        swa_k = apply_rope(swa_k, cos_h[:, None], sin_h[:, None])
        kr = apply_rope(kr, cos_r, sin_r)
    return (na_k, na_v, swa_k, swa_v, ckv, kr)


def query_heads(u, q_norm_g, w_uq, w_uk, rope):
    B, n, _ = u.shape
    na_q = u[..., OFF_NA_Q:OFF_SWA_Q].reshape(B, n, NA_HEADS, HEAD_DIM)
    swa_q = u[..., OFF_SWA_Q:OFF_MLA_CQ].reshape(B, n, SWA_HEADS, HEAD_DIM)
    cq = rms_norm(u[..., OFF_MLA_CQ:OFF_POOL], q_norm_g)
    q = (cq @ w_uq).reshape(B, n, MLA_HEADS, MLA_NOPE + MLA_ROPE)
    q_nope, q_rope = q[..., :MLA_NOPE], q[..., MLA_NOPE:]
    if rope is not None:
        cos_h, sin_h, cos_r, sin_r = rope
        swa_q = apply_rope(swa_q, cos_h[:, None], sin_h[:, None])
        q_rope = apply_rope(q_rope, cos_r[:, None], sin_r[:, None])
    q_lat = jnp.einsum('bnhd,chd->bnhc', q_nope, w_uk)
    pool_in = u[..., OFF_POOL:OFF_GATE]
    gate_logits = u[..., OFF_GATE:]
    return (na_q, swa_q, q_lat, q_rope, pool_in, gate_logits)


def multiscale_pool(u, w_grp, scale):
    B, n, _ = u.shape
    ug = u.reshape(B, n, POOL_GROUPS, POOL_GROUP_DIM).astype(F32)
    cs = jnp.concatenate([jnp.zeros_like(ug[:, :1]), jnp.cumsum(ug, axis=1)], axis=1)
    t = jnp.arange(n)
    means = []
    for g, w in enumerate(POOL_WINDOWS):
        lo = jnp.clip(t - w // 2, 0, n)
        hi = jnp.clip(t - w // 2 + w, 0, n)
        cnt = (hi - lo).astype(F32)[None, :, None]
        means.append((cs[:, hi, g] - cs[:, lo, g]) / cnt)
    y = (jnp.stack(means, axis=2) - ug).astype(u.dtype)
    y = jnp.einsum('bngc,gcd->bngd', y, w_grp).reshape(B, n, POOL_W)
    return y * scale


def neighbourhood_attention(q, k, v, kc, vc, rpb):
    B, n, H, dh = q.shape
    rows = n // GRID_W
    kr_ = min(NA_ROWS, rows)
    m = kr_ * NA_COLS
    qg = q.reshape(B, rows, GRID_W, H, dh)
    kg = k.reshape(B, rows, GRID_W, H, dh)
    vg = v.reshape(B, rows, GRID_W, H, dh)
    col = jnp.arange(GRID_W)
    col_idx = jnp.clip(col - NA_COLS // 2, 0, GRID_W - NA_COLS)[:, None] + jnp.arange(NA_COLS)[None, :]
    col_off = col_idx - col[:, None] + (NA_COLS - 1)
    scale = dh ** -0.5

    def one_row(r):
        r0 = jnp.clip(r - kr_ // 2, 0, rows - kr_)
        qr = lax.dynamic_index_in_dim(qg, r, axis=1, keepdims=False)
        kb = lax.dynamic_slice_in_dim(kg, r0, kr_, axis=1)[:, :, col_idx]
        vb = lax.dynamic_slice_in_dim(vg, r0, kr_, axis=1)[:, :, col_idx]
        row_off = r0 + jnp.arange(kr_) - r + (NA_ROWS - 1)
        bias = rpb[:, row_off[None, :, None], col_off[:, None, :]]
        s_loc = jnp.einsum('bqhd,bjqkhd->bhqjk', qr, kb).astype(F32) * scale + bias.astype(F32)
        s_ctx = jnp.einsum('bqhd,blhd->bhql', qr, kc).astype(F32) * scale
        p = jax.nn.softmax(jnp.concatenate([s_loc.reshape(B, H, GRID_W, m), s_ctx], axis=-1), axis=-1)
        p = p.astype(v.dtype)
        return (jnp.einsum('bhqjk,bjqkhd->bqhd', p[..., :m].reshape(B, H, GRID_W, kr_, NA_COLS), vb)
                + jnp.einsum('bhql,blhd->bqhd', p[..., m:], vc))

    o = lax.map(one_row, jnp.arange(rows))
    return jnp.moveaxis(o, 0, 1).reshape(B, n, H * dh)


def banded(t, nb):
    B, _, KVH, dh = t.shape
    tb = t.reshape(B, nb, SWA_BLOCK, KVH, dh)
    z = jnp.zeros_like(tb[:, :1])
    prev = jnp.concatenate([z, tb[:, :-1]], axis=1)
    nxt = jnp.concatenate([tb[:, 1:], z], axis=1)
    return jnp.concatenate([prev, tb, nxt], axis=2)


def windowed_attention(q, k, v, kc, vc, sink):
    B, n, H, dh = q.shape
    KVH = k.shape[2]
    G = H // KVH
    L = kc.shape[1]
    nb = n // SWA_BLOCK
    m = 3 * SWA_BLOCK
    scale = dh ** -0.5
    qb = q.reshape(B, nb, SWA_BLOCK, KVH, G, dh)
    kb = banded(k, nb)
    vb = banded(v, nb)
    qpos = jnp.arange(SWA_BLOCK)
    kpos = jnp.arange(m) - SWA_BLOCK
    kabs = jnp.arange(nb)[:, None] * SWA_BLOCK + kpos[None, :]
    rel = kpos[None, :] - qpos[:, None]
    mask = (jnp.abs(rel) <= SWA_WINDOW)[None] & ((kabs >= 0) & (kabs < n))[:, None, :]
    s_loc = jnp.einsum('bnqkgd,bnjkd->bnkgqj', qb, kb).astype(F32) * scale
    s_loc = jnp.where(mask[None, :, None, None], s_loc, NEG_INF)
    s_ctx = jnp.einsum('bnqkgd,blkd->bnkgql', qb, kc).astype(F32) * scale
    s_sink = jnp.broadcast_to(sink.astype(F32).reshape(1, 1, KVH, G, 1, 1), s_ctx.shape[:-1] + (1,))
    p = jax.nn.softmax(jnp.concatenate([s_loc, s_ctx, s_sink], axis=-1), axis=-1).astype(v.dtype)
    o = (jnp.einsum('bnkgqj,bnjkd->bnqkgd', p[..., :m], vb)
         + jnp.einsum('bnkgql,blkd->bnqkgd', p[..., m:m + L], vc))
    return o.reshape(B, n, H * dh)


def dense_attention(q, k, v, sink):
    B, n, H, dh = q.shape
    KVH = k.shape[2]
    G = H // KVH
    qg = q.reshape(B, n, KVH, G, dh)
    s = jnp.einsum('bqkgd,bjkd->bkgqj', qg, k).astype(F32) * (dh ** -0.5)
    m = s.shape[-1]
    if sink is not None:
        s = jnp.concatenate([s, jnp.broadcast_to(sink.astype(F32).reshape(1, KVH, G, 1, 1), s.shape[:-1] + (1,))], axis=-1)
    p = jax.nn.softmax(s, axis=-1).astype(v.dtype)[..., :m]
    return jnp.einsum('bkgqj,bjkd->bqkgd', p, v).reshape(B, n, H * dh)


def mla_attend(q_lat, q_rope, ckv, kr, w_uv):
    B, n, H, _ = q_lat.shape
    s = (jnp.einsum('bqhc,bjc->bhqj', q_lat, ckv) + jnp.einsum('bqhr,bjr->bhqj', q_rope, kr)).astype(F32) * MLA_SCALE
    p = jax.nn.softmax(s, axis=-1).astype(ckv.dtype)
    o_lat = jnp.einsum('bhqj,bjc->bqhc', p, ckv)
    return jnp.einsum('bqhc,chd->bqhd', o_lat, w_uv).reshape(B, n, H * MLA_V)


def mla_blocks(q_lat, q_rope, ckv, kr, w_uv):
    B, n, H, C = q_lat.shape
    nb = n // MLA_Q_BLOCK
    qb = jnp.moveaxis(q_lat.reshape(B, nb, MLA_Q_BLOCK, H, C), 1, 0)
    rb = jnp.moveaxis(q_rope.reshape(B, nb, MLA_Q_BLOCK, H, MLA_ROPE), 1, 0)
    o = lax.map(lambda a: mla_attend(a[0], a[1], ckv, kr, w_uv), (qb, rb))
    return jnp.moveaxis(o, 0, 1).reshape(B, n, H * MLA_V)


def latent_branches(q, kv, kv_c, pool_w, pool_scale, rpb, sink, w_uv):
    na_q, swa_q, q_lat, q_rope, pool_in, _ = q
    na_k, na_v, swa_k, swa_v, ckv, kr = kv
    na_kc, na_vc, swa_kc, swa_vc, ckv_c, kr_c = kv_c
    return (multiscale_pool(pool_in, pool_w, pool_scale),
            neighbourhood_attention(na_q, na_k, na_v, na_kc, na_vc, rpb),
            windowed_attention(swa_q, swa_k, swa_v, swa_kc, swa_vc, sink),
            mla_blocks(q_lat, q_rope, jnp.concatenate([ckv_c, ckv], axis=1),
                       jnp.concatenate([kr_c, kr], axis=1), w_uv))


def context_branches(q, kv_c, pool_w, pool_scale, sink, w_uv):
    na_q, swa_q, q_lat, q_rope, pool_in, _ = q
    na_kc, na_vc, swa_kc, swa_vc, ckv_c, kr_c = kv_c
    return (multiscale_pool(pool_in, pool_w, pool_scale),
            dense_attention(na_q, na_kc, na_vc, None),
            dense_attention(swa_q, swa_kc, swa_vc, sink),
            mla_attend(q_lat, q_rope, ckv_c, kr_c, w_uv))


def merge_branches(branches, gate_logits, w_branch, w_out):
    merged = None
    for i, y in enumerate(branches):
        term = jax.nn.sigmoid(gate_logits[..., i * D_MODEL:(i + 1) * D_MODEL]) * (y @ w_branch[i])
        merged = term if merged is None else merged + term
    return merged @ w_out


def swiglu(h, w1, w3, w2):
    return (jax.nn.silu(h @ w1) * (h @ w3)) @ w2


def setup_inputs(seed: int = 0) -> dict:
    key = jax.random.key(seed)
    ks = jax.random.split(key, 24)
    D = D_MODEL
    L = DEPTH

    def nrm(k, shape, s):
        return jax.random.normal(k, shape, F32) * s

    return {
        "x": nrm(ks[0], (BATCH, SEQ, D), 1.0),
        "c": nrm(ks[1], (BATCH, D), 1.0),
        "ctx": nrm(ks[2], (BATCH, CTX_LEN, D), 1.0),
        "c_ctx": nrm(ks[3], (D,), 1.0),
        "ada_w": nrm(ks[4], (L, D, 6 * D), D ** -0.5),
        "ada_b": nrm(ks[5], (L, 6 * D), 0.01),
        "norm1_g": 1.0 + nrm(ks[6], (L, D), 0.1),
        "norm2_g": 1.0 + nrm(ks[7], (L, D), 0.1),
        "w_in": nrm(ks[8], (L, D, IN_COLS), D ** -0.5),
        "pool_w": nrm(ks[9], (L, POOL_GROUPS, POOL_GROUP_DIM, POOL_GROUP_DIM), POOL_GROUP_DIM ** -0.5),
        "pool_scale": 1.0 + nrm(ks[10], (L, POOL_W), 0.1),
        "na_rpb": nrm(ks[11], (L, NA_HEADS, 2 * NA_ROWS - 1, 2 * NA_COLS - 1), 0.1),
        "swa_sink": nrm(ks[12], (L, SWA_HEADS), 0.5),
        "mla_q_norm": 1.0 + nrm(ks[13], (L, MLA_Q_RANK), 0.1),
        "mla_kv_norm": 1.0 + nrm(ks[14], (L, MLA_KV_RANK), 0.1),
        "mla_w_uq": nrm(ks[15], (L, MLA_Q_RANK, MLA_HEADS * (MLA_NOPE + MLA_ROPE)), MLA_Q_RANK ** -0.5),
        "mla_w_uk": nrm(ks[16], (L, MLA_KV_RANK, MLA_HEADS, MLA_NOPE), MLA_KV_RANK ** -0.5),
        "mla_w_uv": nrm(ks[17], (L, MLA_KV_RANK, MLA_HEADS, MLA_V), MLA_KV_RANK ** -0.5),
        "w_branch": nrm(ks[18], (L, N_BRANCH, BRANCH_W, D), BRANCH_W ** -0.5),
        "w_out": nrm(ks[19], (L, D, D), D ** -0.5),
        "ffn_w1": nrm(ks[20], (L, D, FFN_HIDDEN), D ** -0.5),
        "ffn_w3": nrm(ks[21], (L, D, FFN_HIDDEN), D ** -0.5),
        "ffn_w2": nrm(ks[22], (L, FFN_HIDDEN, D), FFN_HIDDEN ** -0.5),
        "final_norm_g": 1.0 + nrm(ks[23], (D,), 0.1),
    }


def reference(x, c, ctx, c_ctx, ada_w, ada_b, norm1_g, norm2_g, w_in, pool_w, pool_scale,
              na_rpb, swa_sink, mla_q_norm, mla_kv_norm, mla_w_uq, mla_w_uk, mla_w_uv,
              w_branch, w_out, ffn_w1, ffn_w3, ffn_w2, final_norm_g):
    n = x.shape[1]
    cos_h, sin_h = axial_rope(n, HEAD_DIM)
    cos_r, sin_r = axial_rope(n, MLA_ROPE)
    rope = (cos_h, sin_h, cos_r, sin_r)
    xc = ctx
    for l in range(DEPTH):
        last = l == DEPTH - 1
        sh1, sc1, g1, sh2, sc2, g2 = modulation(c, ada_w[l], ada_b[l], 6)
        if last:
            csh1, csc1 = modulation(c_ctx, ada_w[l], ada_b[l], 2)
        else:
            csh1, csc1, cg1, csh2, csc2, cg2 = modulation(c_ctx, ada_w[l], ada_b[l], 6)
        hx = modulate(x, norm1_g[l], sh1, sc1)
        hc = modulate(xc, norm1_g[l], csh1, csc1)
        ux = hx @ w_in[l]
        uc = hc @ (w_in[l][:, :KV_COLS] if last else w_in[l])
        kv_c = kv_heads(uc[..., :KV_COLS], mla_kv_norm[l], None)
        kv_x = kv_heads(ux[..., :KV_COLS], mla_kv_norm[l], rope)
        q_x = query_heads(ux, mla_q_norm[l], mla_w_uq[l], mla_w_uk[l], rope)
        br_x = latent_branches(q_x, kv_x, kv_c, pool_w[l], pool_scale[l], na_rpb[l], swa_sink[l], mla_w_uv[l])
        mixed_x = merge_branches(br_x, q_x[5], w_branch[l], w_out[l])
        if not last:
            q_c = query_heads(uc, mla_q_norm[l], mla_w_uq[l], mla_w_uk[l], None)
            br_c = context_branches(q_c, kv_c, pool_w[l], pool_scale[l], swa_sink[l], mla_w_uv[l])
            xc = xc + cg1 * merge_branches(br_c, q_c[5], w_branch[l], w_out[l])
            xc = xc + cg2 * swiglu(modulate(xc, norm2_g[l], csh2, csc2), ffn_w1[l], ffn_w3[l], ffn_w2[l])
        x = x + g1[:, None, :] * mixed_x
        x = x + g2[:, None, :] * swiglu(modulate(x, norm2_g[l], sh2, sc2), ffn_w1[l], ffn_w3[l], ffn_w2[l])
    return rms_norm(x, final_norm_g)
```

```python
import functools

import numpy as np
import jax
import jax.numpy as jnp
from jax import lax
from jax.experimental import pallas as pl
from jax.experimental.pallas import tpu as pltpu

F32 = jnp.float32
BF16 = jnp.bfloat16

GRID_W = 64
HEAD_DIM = 64
ROPE_BASE = 10000.0
NEG_INF = -1e30
EPS = 1e-6
POOL_WINDOWS = (2, 4, 8, 16)
POOL_GROUP_DIM = 64
N_HEADS = 4
NA_ROWS = 8
NA_COLS = 16
SWA_WINDOW = 128
MLA_KV_RANK = 128
MLA_NOPE = 64
MLA_ROPE = 32
MLA_SCALE = (MLA_NOPE + MLA_ROPE) ** -0.5
ATT_SCALE = HEAD_DIM ** -0.5
BRANCH_W = 256
N_BRANCH = 4

OFF_NA_K, OFF_NA_V, OFF_SWA_K, OFF_SWA_V, OFF_MLA_CKV, OFF_MLA_KR = 0, 256, 512, 640, 768, 896
KV_COLS = 928
OFF_NA_Q, OFF_SWA_Q, OFF_MLA_CQ, OFF_POOL, OFF_GATE = 928, 1184, 1440, 1696, 1952

LANES = 128
ATT_TILE = 256
MLA_TQ = 256
MLA_TK = 512
POOL_HALO = 8
VMEM_LIMIT = 56 * 1024 * 1024


def _params(sem):
    return pltpu.CompilerParams(dimension_semantics=sem, vmem_limit_bytes=VMEM_LIMIT)


def _dot(a, b):
    return jnp.dot(a, b, preferred_element_type=F32)


def _dot_nt(a, b):
    return lax.dot_general(a, b, (((1,), (1,)), ((), ())), preferred_element_type=F32)


def _resident(shape):
    n = len(shape)
    return pl.BlockSpec(shape, lambda *_: (0,) * n, pipeline_mode=pl.Buffered(1))


def _modulated_norm(x, g, shift, scale):
    y = x * lax.rsqrt(jnp.mean(x * x, axis=-1, keepdims=True) + EPS)
    return (y * g) * (1.0 + scale) + shift


def _rms(x, g):
    return x * lax.rsqrt(jnp.mean(x * x, axis=-1, keepdims=True) + EPS) * g


def _rope_lanes(t, cos, sin_signed, half):
    lane = lax.broadcasted_iota(jnp.int32, t.shape, 1)
    up = pltpu.roll(t, LANES - half, 1)
    down = pltpu.roll(t, half, 1)
    partner = jnp.where((lane & (2 * half - 1)) < half, up, down)
    return t * cos + partner * sin_signed


def _mod_kernel(c_ref, w_ref, b_ref, o_ref):
    c = c_ref[...]
    s = (c * jax.nn.sigmoid(c)).astype(BF16)
    o_ref[0] = _dot(s, w_ref[0]) + b_ref[0]


def _modulation(cond, ada_w, ada_b):
    L, D, N = ada_w.shape
    tn = N // 4
    return pl.pallas_call(
        _mod_kernel,
        out_shape=jax.ShapeDtypeStruct((L, cond.shape[0], N), F32),
        grid=(L, N // tn),
        in_specs=[pl.BlockSpec(cond.shape, lambda l, j: (0, 0)),
                  pl.BlockSpec((1, D, tn), lambda l, j: (l, 0, j)),
                  pl.BlockSpec((1, 1, tn), lambda l, j: (l, 0, j))],
        out_specs=pl.BlockSpec((1, cond.shape[0], tn), lambda l, j: (l, 0, j)),
        compiler_params=_params(("arbitrary", "arbitrary")),
        name="modulation",
    )(cond, ada_w, ada_b)


def _inproj_kernel(rope, x_ref, mod_ref, g_ref, w_ref, wuq_ref, wuk_ref, qg_ref, kvg_ref, *rest):
    if rope:
        c64_ref, s64_ref, c32_ref, s32_ref = rest[:4]
        rest = rest[4:]
    (naq_ref, nak_ref, nav_ref, swq_ref, swk_ref, swv_ref, kcat_ref, ckvt_ref, qmla_ref, pool_ref) = rest

    x = x_ref[0]
    h = _modulated_norm(x, g_ref[...], mod_ref[0, 0:1, :], mod_ref[0, 1:2, :]).astype(BF16)

    def seg(a, n):
        return _dot(h, w_ref[:, a:a + n])

    def rope64(t):
        if not rope:
            return t
        c, s = c64_ref[...], s64_ref[...]
        return jnp.concatenate([_rope_lanes(t[:, :LANES], c, s, 32), _rope_lanes(t[:, LANES:], c, s, 32)], axis=1)

    def rope32(t):
        return _rope_lanes(t, c32_ref[...], s32_ref[...], 16) if rope else t

    naq_ref[0] = (seg(0, 256) * ATT_SCALE).astype(BF16)
    nak_ref[0] = seg(256, 256).astype(BF16)
    nav_ref[0] = seg(512, 256).astype(BF16)
    swq_ref[0] = (rope64(seg(768, 256)) * ATT_SCALE).astype(BF16)
    swk_ref[0] = rope64(seg(1024, 256)).astype(BF16)
    swv_ref[0] = seg(1280, 256).astype(BF16)

    ckv = _rms(seg(1536, 128), kvg_ref[...])
    kcat_ref[0, :, 0:LANES] = ckv.astype(BF16)
    kcat_ref[0, :, LANES:2 * LANES] = rope32(seg(1664, 128)).astype(BF16)
    ckvt_ref[0] = ckv.T.astype(BF16)

    cq = _rms(seg(1792, 256), qg_ref[...]).astype(BF16)
    qq = _dot(cq, wuq_ref[...])
    q_lat = _dot(qq[:, :256].astype(BF16), wuk_ref[...])
    q_rope = rope32(qq[:, 256:384])
    lane = lax.broadcasted_iota(jnp.int32, q_rope.shape, 1)
    for hd in range(N_HEADS):
        qmla_ref[0, hd, :, 0:LANES] = (q_lat[:, hd * LANES:(hd + 1) * LANES] * MLA_SCALE).astype(BF16)
        own = (lane >= hd * MLA_ROPE) & (lane < (hd + 1) * MLA_ROPE)
        qmla_ref[0, hd, :, LANES:2 * LANES] = (jnp.where(own, q_rope, 0.0) * MLA_SCALE).astype(BF16)

    pool_ref[0] = seg(2048, 256)


def _inproj(x, mod, g, wp, wuq, wuk, qg, kvg, rope_tabs, tm):
    B, n, D = x.shape
    rope = rope_tabs is not None
    mod_b = mod.shape[0]
    tok = lambda w: pl.BlockSpec((1, tm, w), lambda b, i: (b, i, 0))
    in_specs = [tok(D),
                pl.BlockSpec((1, 6, D), (lambda b, i: (b, 0, 0)) if mod_b > 1 else (lambda b, i: (0, 0, 0))),
                _resident(g.shape), _resident(wp.shape), _resident(wuq.shape), _resident(wuk.shape),
                _resident(qg.shape), _resident(kvg.shape)]
    args = [x, mod, g, wp, wuq, wuk, qg, kvg]
    if rope:
        in_specs += [pl.BlockSpec((tm, LANES), lambda b, i: (i, 0))] * 4
        args += list(rope_tabs)
    bf = lambda w: jax.ShapeDtypeStruct((B, n, w), BF16)
    out_shape = [bf(256)] * 7 + [jax.ShapeDtypeStruct((B, MLA_KV_RANK, n), BF16),
                                 jax.ShapeDtypeStruct((B, N_HEADS, n, 256), BF16),
                                 jax.ShapeDtypeStruct((B, n, 256), F32)]
    out_specs = [tok(256)] * 7 + [pl.BlockSpec((1, MLA_KV_RANK, tm), lambda b, i: (b, 0, i)),
                                  pl.BlockSpec((1, N_HEADS, tm, 256), lambda b, i: (b, 0, i, 0)),
                                  tok(256)]
    return pl.pallas_call(
        functools.partial(_inproj_kernel, rope),
        out_shape=out_shape, grid=(B, n // tm), in_specs=in_specs, out_specs=out_specs,
        compiler_params=_params(("arbitrary", "arbitrary")),
        name="inproj_rope" if rope else "inproj_ctx",
    )(*args)


def _pool_kernel(n, tp, u_ref, prev_ref, next_ref, w_ref, sc_ref, o_ref, ext_ref):
    i = pl.program_id(1)
    last = pl.num_programs(1) - 1
    ext_ref[0:POOL_HALO, :] = jnp.where(i > 0, prev_ref[0], 0.0)
    ext_ref[POOL_HALO:POOL_HALO + tp, :] = u_ref[0]
    ext_ref[POOL_HALO + tp:, :] = jnp.where(i < last, next_ref[0], 0.0)

    t = i * tp + lax.broadcasted_iota(jnp.int32, (tp, LANES), 0)
    lane = lax.broadcasted_iota(jnp.int32, (tp, LANES), 1)

    def window_mean(col, w):
        acc = None
        for k in range(-(w // 2), w - w // 2):
            v = ext_ref[POOL_HALO + k:POOL_HALO + k + tp, col * LANES:(col + 1) * LANES]
            acc = v if acc is None else acc + v
        lo = jnp.clip(t - w // 2, 0, n)
        hi = jnp.clip(t - w // 2 + w, 0, n)
        return acc / (hi - lo).astype(F32)

    first = lane < POOL_GROUP_DIM
    m01 = jnp.where(first, window_mean(0, POOL_WINDOWS[0]), window_mean(0, POOL_WINDOWS[1]))
    m23 = jnp.where(first, window_mean(1, POOL_WINDOWS[2]), window_mean(1, POOL_WINDOWS[3]))
    y = (jnp.concatenate([m01, m23], axis=1) - u_ref[0]).astype(BF16)
    o_ref[0] = (_dot(y, w_ref[...]) * sc_ref[...]).astype(BF16)


def _pool(u, w_bd, scale, tp):
    B, n, W = u.shape
    hb = tp // POOL_HALO
    nh = n // POOL_HALO
    return pl.pallas_call(
        functools.partial(_pool_kernel, n, tp),
        out_shape=jax.ShapeDtypeStruct((B, n, W), BF16),
        grid=(B, n // tp),
        in_specs=[pl.BlockSpec((1, tp, W), lambda b, i: (b, i, 0)),
                  pl.BlockSpec((1, POOL_HALO, W), lambda b, i: (b, jnp.maximum(i * hb - 1, 0), 0)),
                  pl.BlockSpec((1, POOL_HALO, W), lambda b, i: (b, jnp.minimum((i + 1) * hb, nh - 1), 0)),
                  _resident(w_bd.shape), _resident(scale.shape)],
        out_specs=pl.BlockSpec((1, tp, W), lambda b, i: (b, i, 0)),
        scratch_shapes=[pltpu.VMEM((tp + 2 * POOL_HALO, W), F32)],
        compiler_params=_params(("arbitrary", "arbitrary")),
        name="pool",
    )(u, u, u, w_bd, scale)


def _head_mask(shape, hd):
    lane = lax.broadcasted_iota(jnp.int32, shape, 1)
    return (lane >= hd * HEAD_DIM) & (lane < (hd + 1) * HEAD_DIM)


def _attend_heads(q, key_parts, val_parts, fix_parts, sink_ref):
    out = jnp.zeros(q.shape, F32)
    for hd in range(N_HEADS):
        qh = jnp.where(_head_mask(q.shape, hd), q, jnp.zeros_like(q))
        scores = []
        for kp, fix in zip(key_parts, fix_parts):
            s = _dot_nt(qh, kp)
            scores.append(s if fix is None else fix(hd, s))
        m = functools.reduce(jnp.maximum, [jnp.max(s, axis=-1, keepdims=True) for s in scores])
        if sink_ref is not None:
            sink = sink_ref[hd:hd + 1, 0:1]
            m = jnp.maximum(m, sink)
        probs = [jnp.exp(s - m) for s in scores]
        den = functools.reduce(jnp.add, [jnp.sum(p, axis=-1, keepdims=True) for p in probs])
        if sink_ref is not None:
            den = den + jnp.exp(sink - m)
        o = functools.reduce(jnp.add, [_dot(p.astype(BF16), vp) for p, vp in zip(probs, val_parts)])
        out = jnp.where(_head_mask(out.shape, hd), o / den, out)
    return out


def _na_kernel(q_ref, k0_ref, k1_ref, k2_ref, v0_ref, v1_ref, v2_ref, kc_ref, vc_ref, bias_ref, o_ref):
    T = ATT_TILE
    fixes = [(lambda hd, s, j=j: s + bias_ref[0, hd, :, j * T:(j + 1) * T]) for j in range(3)] + [None]
    out = _attend_heads(q_ref[0], [k0_ref[0], k1_ref[0], k2_ref[0], kc_ref[0]],
                        [v0_ref[0], v1_ref[0], v2_ref[0], vc_ref[0]], fixes, None)
    o_ref[0] = out.astype(BF16)


def _neighbour_specs(nt):
    T = ATT_TILE
    return [pl.BlockSpec((1, T, 256), lambda b, i: (b, jnp.maximum(i - 1, 0), 0)),
            pl.BlockSpec((1, T, 256), lambda b, i: (b, i, 0)),
            pl.BlockSpec((1, T, 256), lambda b, i: (b, jnp.minimum(i + 1, nt - 1), 0))]


def _na(q, k, v, kc, vc, bias):
    B, S, W = q.shape
    T = ATT_TILE
    nt = S // T
    tile = pl.BlockSpec((1, T, W), lambda b, i: (b, i, 0))
    ctx = pl.BlockSpec((1,) + kc.shape[1:], lambda b, i: (b, 0, 0))
    variant = lambda b, i: (jnp.where(i == 0, 0, jnp.where(i == nt - 1, 2, 1)), 0, 0, 0)
    return pl.pallas_call(
        _na_kernel,
        out_shape=jax.ShapeDtypeStruct((B, S, W), BF16),
        grid=(B, nt),
        in_specs=[tile] + _neighbour_specs(nt) + _neighbour_specs(nt) + [ctx, ctx,
                  pl.BlockSpec((1,) + bias.shape[1:], variant)],
        out_specs=tile,
        compiler_params=_params(("arbitrary", "arbitrary")),
        name="neighbourhood_attention",
    )(q, k, k, k, v, v, v, kc, vc, bias)


def _swa_kernel(S, q_ref, k0_ref, k1_ref, k2_ref, v0_ref, v1_ref, v2_ref, kc_ref, vc_ref, sink_ref, o_ref):
    T = ATT_TILE
    i = pl.program_id(1)
    qpos = i * T + lax.broadcasted_iota(jnp.int32, (T, T), 0)
    col = lax.broadcasted_iota(jnp.int32, (T, T), 1)

    def band(j):
        kpos = (i - 1 + j) * T + col
        ok = (jnp.abs(kpos - qpos) <= SWA_WINDOW) & (kpos >= 0) & (kpos < S)
        return lambda hd, s: jnp.where(ok, s, NEG_INF)

    fixes = [band(0), band(1), band(2), None]
    out = _attend_heads(q_ref[0], [k0_ref[0], k1_ref[0], k2_ref[0], kc_ref[0]],
                        [v0_ref[0], v1_ref[0], v2_ref[0], vc_ref[0]], fixes, sink_ref)
    o_ref[0] = out.astype(BF16)


def _swa(q, k, v, kc, vc, sink_tab):
    B, S, W = q.shape
    T = ATT_TILE
    nt = S // T
    tile = pl.BlockSpec((1, T, W), lambda b, i: (b, i, 0))
    ctx = pl.BlockSpec((1,) + kc.shape[1:], lambda b, i: (b, 0, 0))
    return pl.pallas_call(
        functools.partial(_swa_kernel, S),
        out_shape=jax.ShapeDtypeStruct((B, S, W), BF16),
        grid=(B, nt),
        in_specs=[tile] + _neighbour_specs(nt) + _neighbour_specs(nt) + [ctx, ctx, _resident(sink_tab.shape)],
        out_specs=tile,
        compiler_params=_params(("arbitrary", "arbitrary")),
        name="windowed_attention",
    )(q, k, k, k, v, v, v, kc, vc, sink_tab)


def _mla_finish(acc, den, wuvt_ref, tq):
    o = (acc / den).astype(BF16)
    outs = [_dot(wuvt_ref[hd], o[:, hd * tq:(hd + 1) * tq]) for hd in range(N_HEADS)]
    return jnp.concatenate(outs, axis=0).T


def _mla_kernel(n_chunks, tk, q_ref, k_ref, vt_ref, kc_ref, vtc_ref, wuvt_ref, o_ref, m_ref, l_ref, acc_ref):
    tq = q_ref.shape[2]
    q = q_ref[0].reshape(N_HEADS * tq, q_ref.shape[3])

    s = _dot_nt(kc_ref[0], q)
    m = jnp.max(s, axis=0, keepdims=True)
    p = jnp.exp(s - m)
    m_ref[...] = m
    l_ref[...] = jnp.sum(p, axis=0, keepdims=True)
    acc_ref[...] = _dot(vtc_ref[0], p.astype(BF16))

    def body(c, carry):
        off = pl.multiple_of(c * tk, tk)
        s = _dot_nt(k_ref[0, pl.ds(off, tk), :], q)
        m_old = m_ref[...]
        m_new = jnp.maximum(m_old, jnp.max(s, axis=0, keepdims=True))
        alpha = jnp.exp(m_old - m_new)
        p = jnp.exp(s - m_new)
        m_ref[...] = m_new
        l_ref[...] = alpha * l_ref[...] + jnp.sum(p, axis=0, keepdims=True)
        acc_ref[...] = alpha * acc_ref[...] + _dot(vt_ref[0, :, pl.ds(off, tk)], p.astype(BF16))
        return carry

    lax.fori_loop(0, n_chunks, body, 0)
    o_ref[0] = _mla_finish(acc_ref[...], l_ref[...], wuvt_ref, tq).astype(BF16)


def _mla(q, k, vt, kc, vtc, wuvt):
    B, H, S, W = q.shape
    tq, tk = MLA_TQ, min(MLA_TK, S)
    per_batch = lambda shape: pl.BlockSpec((1,) + shape[1:], lambda b, i: (b,) + (0,) * (len(shape) - 1))
    return pl.pallas_call(
        functools.partial(_mla_kernel, S // tk, tk),
        out_shape=jax.ShapeDtypeStruct((B, S, BRANCH_W), BF16),
        grid=(B, S // tq),
        in_specs=[pl.BlockSpec((1, H, tq, W), lambda b, i: (b, 0, i, 0)),
                  per_batch(k.shape), per_batch(vt.shape), per_batch(kc.shape), per_batch(vtc.shape),
                  _resident(wuvt.shape)],
        out_specs=pl.BlockSpec((1, tq, BRANCH_W), lambda b, i: (b, i, 0)),
        scratch_shapes=[pltpu.VMEM((1, H * tq), F32), pltpu.VMEM((1, H * tq), F32),
                        pltpu.VMEM((MLA_KV_RANK, H * tq), F32)],
        compiler_params=_params(("arbitrary", "arbitrary")),
        name="latent_attention",
    )(q, k, vt, kc, vtc, wuvt)


def _ctx_attn_kernel(naq_ref, nak_ref, nav_ref, swq_ref, swk_ref, swv_ref, qmla_ref, kcat_ref, ckvt_ref,
                     sink_ref, wuvt_ref, na_o_ref, sw_o_ref, mla_o_ref):
    na_o_ref[0] = _attend_heads(naq_ref[0], [nak_ref[0]], [nav_ref[0]], [None], None).astype(BF16)
    sw_o_ref[0] = _attend_heads(swq_ref[0], [swk_ref[0]], [swv_ref[0]], [None], sink_ref).astype(BF16)
    tq = qmla_ref.shape[2]
    q = qmla_ref[0].reshape(N_HEADS * tq, qmla_ref.shape[3])
    s = _dot_nt(kcat_ref[0], q)
    p = jnp.exp(s - jnp.max(s, axis=0, keepdims=True))
    den = jnp.sum(p, axis=0, keepdims=True)
    acc = _dot(ckvt_ref[0], p.astype(BF16))
    mla_o_ref[0] = _mla_finish(acc, den, wuvt_ref, tq).astype(BF16)


def _ctx_attn(naq, nak, nav, swq, swk, swv, qmla, kcat, ckvt, sink_tab, wuvt):
    B, Lc, W = naq.shape
    per_batch = lambda a: pl.BlockSpec((1,) + a.shape[1:], lambda b: (b,) + (0,) * (a.ndim - 1))
    acts = [naq, nak, nav, swq, swk, swv, qmla, kcat, ckvt]
    out = jax.ShapeDtypeStruct((B, Lc, W), BF16)
    return pl.pallas_call(
        _ctx_attn_kernel,
        out_shape=[out, out, out],
        grid=(B,),
        in_specs=[per_batch(a) for a in acts] + [_resident(sink_tab.shape), _resident(wuvt.shape)],
        out_specs=[per_batch(naq)] * 3,
        compiler_params=_params(("arbitrary",)),
        name="context_attention",
    )(*acts, sink_tab, wuvt)


def _merge_kernel(x_ref, mod_ref, g_ref, y0_ref, y1_ref, y2_ref, y3_ref, wg_ref, wb_ref, wo_ref, o_ref):
    x = x_ref[0]
    D = x.shape[1]
    h = _modulated_norm(x, g_ref[...], mod_ref[0, 0:1, :], mod_ref[0, 1:2, :]).astype(BF16)
    merged = None
    for i, y_ref in enumerate((y0_ref, y1_ref, y2_ref, y3_ref)):
        gate = jax.nn.sigmoid(_dot(h, wg_ref[:, i * D:(i + 1) * D]))
        term = gate * _dot(y_ref[0], wb_ref[i])
        merged = term if merged is None else merged + term
    o_ref[0] = x + mod_ref[0, 2:3, :] * _dot(merged.astype(BF16), wo_ref[...])


def _mod_spec(mod, D):
    return pl.BlockSpec((1, 6, D), (lambda b, i: (b, 0, 0)) if mod.shape[0] > 1 else (lambda b, i: (0, 0, 0)))


def _merge(x, mod, g, ys, wg, wb, wo, tm):
    B, n, D = x.shape
    tok = lambda w: pl.BlockSpec((1, tm, w), lambda b, i: (b, i, 0))
    return pl.pallas_call(
        _merge_kernel,
        out_shape=jax.ShapeDtypeStruct((B, n, D), F32),
        grid=(B, n // tm),
        in_specs=[tok(D), _mod_spec(mod, D), _resident(g.shape)] + [tok(BRANCH_W)] * 4
                 + [_resident(wg.shape), _resident(wb.shape), _resident(wo.shape)],
        out_specs=tok(D),
        compiler_params=_params(("arbitrary", "arbitrary")),
        name="merge",
    )(x, mod, g, *ys, wg, wb, wo)


def _ffn_kernel(final, x_ref, mod_ref, g_ref, w1_ref, w3_ref, w2_ref, *rest):
    if final:
        fg_ref, o_ref = rest
    else:
        (o_ref,) = rest
    x = x_ref[0]
    h = _modulated_norm(x, g_ref[...], mod_ref[0, 3:4, :], mod_ref[0, 4:5, :]).astype(BF16)
    a = _dot(h, w1_ref[...])
    act = ((a * jax.nn.sigmoid(a)) * _dot(h, w3_ref[...])).astype(BF16)
    y = x + mod_ref[0, 5:6, :] * _dot(act, w2_ref[...])
    o_ref[0] = _rms(y, fg_ref[...]) if final else y


def _ffn(x, mod, g, w1, w3, w2, final_g, tm):
    B, n, D = x.shape
    final = final_g is not None
    tok = pl.BlockSpec((1, tm, D), lambda b, i: (b, i, 0))
    in_specs = [tok, _mod_spec(mod, D), _resident(g.shape), _resident(w1.shape), _resident(w3.shape),
                _resident(w2.shape)]
    args = [x, mod, g, w1, w3, w2]
    if final:
        in_specs.append(_resident(final_g.shape))
        args.append(final_g)
    return pl.pallas_call(
        functools.partial(_ffn_kernel, final),
        out_shape=jax.ShapeDtypeStruct((B, n, D), F32),
        grid=(B, n // tm),
        in_specs=in_specs, out_specs=tok,
        compiler_params=_params(("arbitrary", "arbitrary")),
        name="ffn_final" if final else "ffn",
    )(*args)


def _pack_w_in(w):
    def dup_heads(a):
        return jnp.concatenate([a[:, :64], a[:, :64], a[:, 64:], a[:, 64:]], axis=1)

    kr = w[:, OFF_MLA_KR:KV_COLS]
    cols = [w[:, OFF_NA_Q:OFF_NA_Q + 256], w[:, OFF_NA_K:OFF_NA_K + 256], w[:, OFF_NA_V:OFF_NA_V + 256],
            w[:, OFF_SWA_Q:OFF_SWA_Q + 256], dup_heads(w[:, OFF_SWA_K:OFF_SWA_V]),
            dup_heads(w[:, OFF_SWA_V:OFF_MLA_CKV]), w[:, OFF_MLA_CKV:OFF_MLA_KR],
            jnp.concatenate([kr] * N_HEADS, axis=1), w[:, OFF_MLA_CQ:OFF_POOL], w[:, OFF_POOL:OFF_GATE]]
    return jnp.concatenate(cols, axis=1).astype(BF16)


def _block_diag(blocks):
    r, c = blocks[0].shape
    n = len(blocks)
    rows = [jnp.concatenate([blocks[i] if j == i else jnp.zeros((r, c), blocks[0].dtype) for j in range(n)], axis=1)
            for i in range(n)]
    return jnp.concatenate(rows, axis=0)


def _rope_tables(n, dim):
    t = jnp.arange(n, dtype=jnp.int32)
    row = (t // GRID_W).astype(F32)
    col = (t % GRID_W).astype(F32)
    n_freq = dim // 4
    inv = jnp.power(ROPE_BASE, -jnp.arange(n_freq, dtype=F32) / n_freq)
    ang = jnp.concatenate([row[:, None] * inv, col[:, None] * inv], axis=-1)
    cos, sin = jnp.cos(ang), jnp.sin(ang)
    reps = LANES // dim
    cos_t = jnp.tile(jnp.concatenate([cos, cos], axis=1), (1, reps))
    sin_t = jnp.tile(jnp.concatenate([-sin, sin], axis=1), (1, reps))
    return cos_t, sin_t


def _na_bias_tables(rpb, rows):
    T = ATT_TILE
    tr = T // GRID_W
    nt = rows // tr
    qi = np.arange(T)
    kk = np.arange(3 * T)
    qc, kc = qi % GRID_W, kk % GRID_W
    c0 = np.clip(qc - NA_COLS // 2, 0, GRID_W - NA_COLS)
    col_ok = (kc[None, :] >= c0[:, None]) & (kc[None, :] < c0[:, None] + NA_COLS)
    col_off = np.clip(kc[None, :] - qc[:, None] + (NA_COLS - 1), 0, 2 * NA_COLS - 2)
    tabs = []
    for i in (0, 1, nt - 1):
        qr = tr * i + qi // GRID_W
        kr = tr * (i - 1) + kk // GRID_W
        r0 = np.clip(qr - NA_ROWS // 2, 0, rows - NA_ROWS)
        row_ok = (kr[None, :] >= r0[:, None]) & (kr[None, :] < r0[:, None] + NA_ROWS)
        row_off = np.clip(kr[None, :] - qr[:, None] + (NA_ROWS - 1), 0, 2 * NA_ROWS - 2)
        vals = rpb[:, row_off, col_off]
        tabs.append(jnp.where(jnp.asarray(row_ok & col_ok)[None], vals, NEG_INF))
    return jnp.stack(tabs, axis=0).astype(F32)


def kernel(x, c, ctx, c_ctx, ada_w, ada_b, norm1_g, norm2_g, w_in, pool_w, pool_scale, na_rpb, swa_sink,
           mla_q_norm, mla_kv_norm, mla_w_uq, mla_w_uk, mla_w_uv, w_branch, w_out, ffn_w1, ffn_w3, ffn_w2,
           final_norm_g):
    B, S, D = x.shape
    Lc = ctx.shape[1]
    depth = ada_w.shape[0]
    rows = S // GRID_W
    assert S % 1024 == 0 and rows >= 3 * (ATT_TILE // GRID_W) and Lc == ATT_TILE and B <= 7
    tm_x, tm_c = 512, Lc

    cond = jnp.concatenate([c, c_ctx[None], jnp.zeros((8 - B - 1, D), F32)], axis=0)
    mods = _modulation(cond, ada_w.astype(BF16), ada_b[:, None, :])
    rope_tabs = _rope_tables(S, HEAD_DIM) + _rope_tables(S, MLA_ROPE)
    row2 = lambda v: v.reshape(1, -1)

    xc = ctx
    for l in range(depth):
        last = l == depth - 1
        mod_x = mods[l, :B].reshape(B, 6, D)
        mod_c = mods[l, B:B + 1].reshape(1, 6, D)
        wp = _pack_w_in(w_in[l])
        wg = w_in[l][:, OFF_GATE:].astype(BF16)
        uq = mla_w_uq[l].reshape(-1, N_HEADS, MLA_NOPE + MLA_ROPE)
        wuq = jnp.concatenate([uq[:, :, :MLA_NOPE].reshape(-1, N_HEADS * MLA_NOPE),
                               uq[:, :, MLA_NOPE:].reshape(-1, N_HEADS * MLA_ROPE)], axis=1).astype(BF16)
        uk = jnp.transpose(mla_w_uk[l], (1, 2, 0))
        wuk = _block_diag([uk[h] for h in range(N_HEADS)]).astype(BF16)
        wuvt = jnp.transpose(mla_w_uv[l], (1, 2, 0)).astype(BF16)
        w_pool = _block_diag([pool_w[l, g] for g in range(len(POOL_WINDOWS))]).astype(BF16)
        sink_tab = jnp.concatenate([jnp.broadcast_to(swa_sink[l][:, None], (N_HEADS, LANES)),
                                    jnp.zeros((8 - N_HEADS, LANES), F32)], axis=0)
        bias = _na_bias_tables(na_rpb[l], rows)
        g1, g2 = row2(norm1_g[l]), row2(norm2_g[l])
        qg, kvg, psc = row2(mla_q_norm[l]), row2(mla_kv_norm[l]), row2(pool_scale[l])
        wb, wo = w_branch[l].astype(BF16), w_out[l].astype(BF16)
        w1, w3, w2 = ffn_w1[l].astype(BF16), ffn_w3[l].astype(BF16), ffn_w2[l].astype(BF16)

        (naq, nak, nav, swq, swk, swv, kcat, ckvt, qmla, pool_in) = _inproj(
            x, mod_x, g1, wp, wuq, wuk, qg, kvg, rope_tabs, tm_x)
        (c_naq, c_nak, c_nav, c_swq, c_swk, c_swv, c_kcat, c_ckvt, c_qmla, c_pool_in) = _inproj(
            xc, mod_c, g1, wp, wuq, wuk, qg, kvg, None, tm_c)

        ys = [_pool(pool_in, w_pool, psc, 512),
              _na(naq, nak, nav, c_nak, c_nav, bias),
              _swa(swq, swk, swv, c_swk, c_swv, sink_tab),
              _mla(qmla, kcat, ckvt, c_kcat, c_ckvt, wuvt)]
        if not last:
            ys_c = [_pool(c_pool_in, w_pool, psc, Lc)] + list(_ctx_attn(
                c_naq, c_nak, c_nav, c_swq, c_swk, c_swv, c_qmla, c_kcat, c_ckvt, sink_tab, wuvt))
            xc = _merge(xc, mod_c, g1, ys_c, wg, wb, wo, tm_c)
            xc = _ffn(xc, mod_c, g2, w1, w3, w2, None, tm_c)
        x = _merge(x, mod_x, g1, ys, wg, wb, wo, tm_x)
        x = _ffn(x, mod_x, g2, w1, w3, w2, row2(final_norm_g) if last else None, tm_x)
    return x
```

```python
import functools

import numpy as np
import jax
import jax.numpy as jnp
from jax import lax
from jax.experimental import pallas as pl
from jax.experimental.pallas import tpu as pltpu

F32 = jnp.float32
BF16 = jnp.bfloat16

GRID_W = 64
HEAD_DIM = 64
ROPE_BASE = 10000.0
NEG_INF = -1e30
EPS = 1e-6
POOL_WINDOWS = (2, 4, 8, 16)
POOL_GROUP_DIM = 64
N_HEADS = 4
NA_ROWS = 8
NA_COLS = 16
SWA_WINDOW = 128
MLA_KV_RANK = 128
MLA_NOPE = 64
MLA_ROPE = 32
MLA_SCALE = (MLA_NOPE + MLA_ROPE) ** -0.5
LOG2_E = 1.4426950408889634
MLA_Q_SCALE = MLA_SCALE * LOG2_E
MLA_SUM_ROWS = 16
ATT_SCALE = HEAD_DIM ** -0.5
BRANCH_W = 256
N_BRANCH = 4

OFF_NA_K, OFF_NA_V, OFF_SWA_K, OFF_SWA_V, OFF_MLA_CKV, OFF_MLA_KR = 0, 256, 512, 640, 768, 896
KV_COLS = 928
OFF_NA_Q, OFF_SWA_Q, OFF_MLA_CQ, OFF_POOL, OFF_GATE = 928, 1184, 1440, 1696, 1952

LANES = 128
ATT_TILE = 256
MLA_TQ = 256
MLA_TK = 512
POOL_HALO = 8
VMEM_LIMIT = 56 * 1024 * 1024


def _params(sem):
    return pltpu.CompilerParams(dimension_semantics=sem, vmem_limit_bytes=VMEM_LIMIT)


def _dot(a, b):
    return jnp.dot(a, b, preferred_element_type=F32)


def _dot_nt(a, b):
    return lax.dot_general(a, b, (((1,), (1,)), ((), ())), preferred_element_type=F32)


def _resident(shape):
    n = len(shape)
    return pl.BlockSpec(shape, lambda *_: (0,) * n, pipeline_mode=pl.Buffered(1))


def _modulated_norm(x, g, shift, scale):
    y = x * lax.rsqrt(jnp.mean(x * x, axis=-1, keepdims=True) + EPS)
    return (y * g) * (1.0 + scale) + shift


def _rms(x, g):
    return x * lax.rsqrt(jnp.mean(x * x, axis=-1, keepdims=True) + EPS) * g


def _rope_lanes(t, cos, sin_signed, half):
    lane = lax.broadcasted_iota(jnp.int32, t.shape, 1)
    up = pltpu.roll(t, LANES - half, 1)
    down = pltpu.roll(t, half, 1)
    partner = jnp.where((lane & (2 * half - 1)) < half, up, down)
    return t * cos + partner * sin_signed


def _mod_kernel(c_ref, w_ref, b_ref, o_ref):
    c = c_ref[...]
    s = (c * jax.nn.sigmoid(c)).astype(BF16)
    o_ref[0] = _dot(s, w_ref[0]) + b_ref[0]


def _modulation(cond, ada_w, ada_b):
    L, D, N = ada_w.shape
    tn = N // 4
    return pl.pallas_call(
        _mod_kernel,
        out_shape=jax.ShapeDtypeStruct((L, cond.shape[0], N), F32),
        grid=(L, N // tn),
        in_specs=[pl.BlockSpec(cond.shape, lambda l, j: (0, 0)),
                  pl.BlockSpec((1, D, tn), lambda l, j: (l, 0, j)),
                  pl.BlockSpec((1, 1, tn), lambda l, j: (l, 0, j))],
        out_specs=pl.BlockSpec((1, cond.shape[0], tn), lambda l, j: (l, 0, j)),
        compiler_params=_params(("arbitrary", "arbitrary")),
        name="modulation",
    )(cond, ada_w, ada_b)


def _inproj_kernel(rope, x_ref, mod_ref, g_ref, w_ref, wuq_ref, wuk_ref, qg_ref, kvg_ref, *rest):
    if rope:
        c64_ref, s64_ref, c32_ref, s32_ref = rest[:4]
        rest = rest[4:]
    (naq_ref, nak_ref, nav_ref, swq_ref, swk_ref, swv_ref, kcat_ref, ckvt_ref, qmla_ref, pool_ref) = rest

    x = x_ref[0]
    h = _modulated_norm(x, g_ref[...], mod_ref[0, 0:1, :], mod_ref[0, 1:2, :]).astype(BF16)

    def seg(a, n):
        return _dot(h, w_ref[:, a:a + n])

    def rope64(t):
        if not rope:
            return t
        c, s = c64_ref[...], s64_ref[...]
        return jnp.concatenate([_rope_lanes(t[:, :LANES], c, s, 32), _rope_lanes(t[:, LANES:], c, s, 32)], axis=1)

    def rope32(t):
        return _rope_lanes(t, c32_ref[...], s32_ref[...], 16) if rope else t

    naq_ref[0] = (seg(0, 256) * ATT_SCALE).astype(BF16)
    nak_ref[0] = seg(256, 256).astype(BF16)
    nav_ref[0] = seg(512, 256).astype(BF16)
    swq_ref[0] = (rope64(seg(768, 256)) * ATT_SCALE).astype(BF16)
    swk_ref[0] = rope64(seg(1024, 256)).astype(BF16)
    swv_ref[0] = seg(1280, 256).astype(BF16)

    ckv = _rms(seg(1536, 128), kvg_ref[...])
    kcat_ref[0, :, 0:LANES] = ckv.astype(BF16)
    kcat_ref[0, :, LANES:2 * LANES] = rope32(seg(1664, 128)).astype(BF16)
    ckvt_ref[0, 0:MLA_KV_RANK, :] = ckv.T.astype(BF16)
    ckvt_ref[0, MLA_KV_RANK:, :] = jnp.ones((MLA_SUM_ROWS, ckv.shape[0]), BF16)

    cq = _rms(seg(1792, 256), qg_ref[...]).astype(BF16)
    qq = _dot(cq, wuq_ref[...])
    q_lat = _dot(qq[:, :256].astype(BF16), wuk_ref[...])
    q_rope = rope32(qq[:, 256:384])
    lane = lax.broadcasted_iota(jnp.int32, q_rope.shape, 1)
    for hd in range(N_HEADS):
        qmla_ref[0, hd, :, 0:LANES] = (q_lat[:, hd * LANES:(hd + 1) * LANES] * MLA_Q_SCALE).astype(BF16)
        own = (lane >= hd * MLA_ROPE) & (lane < (hd + 1) * MLA_ROPE)
        qmla_ref[0, hd, :, LANES:2 * LANES] = (jnp.where(own, q_rope, 0.0) * MLA_Q_SCALE).astype(BF16)

    pool_ref[0] = seg(2048, 256)


def _inproj(x, mod, g, wp, wuq, wuk, qg, kvg, rope_tabs, tm):
    B, n, D = x.shape
    rope = rope_tabs is not None
    mod_b = mod.shape[0]
    tok = lambda w: pl.BlockSpec((1, tm, w), lambda b, i: (b, i, 0))
    in_specs = [tok(D),
                pl.BlockSpec((1, 6, D), (lambda b, i: (b, 0, 0)) if mod_b > 1 else (lambda b, i: (0, 0, 0))),
                _resident(g.shape), _resident(wp.shape), _resident(wuq.shape), _resident(wuk.shape),
                _resident(qg.shape), _resident(kvg.shape)]
    args = [x, mod, g, wp, wuq, wuk, qg, kvg]
    if rope:
        in_specs += [pl.BlockSpec((tm, LANES), lambda b, i: (i, 0))] * 4
        args += list(rope_tabs)
    bf = lambda w: jax.ShapeDtypeStruct((B, n, w), BF16)
    vt_rows = MLA_KV_RANK + MLA_SUM_ROWS
    out_shape = [bf(256)] * 7 + [jax.ShapeDtypeStruct((B, vt_rows, n), BF16),
                                 jax.ShapeDtypeStruct((B, N_HEADS, n, 256), BF16),
                                 jax.ShapeDtypeStruct((B, n, 256), F32)]
    out_specs = [tok(256)] * 7 + [pl.BlockSpec((1, vt_rows, tm), lambda b, i: (b, 0, i)),
                                  pl.BlockSpec((1, N_HEADS, tm, 256), lambda b, i: (b, 0, i, 0)),
                                  tok(256)]
    return pl.pallas_call(
        functools.partial(_inproj_kernel, rope),
        out_shape=out_shape, grid=(B, n // tm), in_specs=in_specs, out_specs=out_specs,
        compiler_params=_params(("arbitrary", "arbitrary")),
        name="inproj_rope" if rope else "inproj_ctx",
    )(*args)


def _pool_kernel(n, tp, u_ref, prev_ref, next_ref, w_ref, sc_ref, o_ref, ext_ref):
    i = pl.program_id(1)
    last = pl.num_programs(1) - 1
    ext_ref[0:POOL_HALO, :] = jnp.where(i > 0, prev_ref[0], 0.0)
    ext_ref[POOL_HALO:POOL_HALO + tp, :] = u_ref[0]
    ext_ref[POOL_HALO + tp:, :] = jnp.where(i < last, next_ref[0], 0.0)

    t = i * tp + lax.broadcasted_iota(jnp.int32, (tp, LANES), 0)
    lane = lax.broadcasted_iota(jnp.int32, (tp, LANES), 1)

    def window_mean(col, w):
        acc = None
        for k in range(-(w // 2), w - w // 2):
            v = ext_ref[POOL_HALO + k:POOL_HALO + k + tp, col * LANES:(col + 1) * LANES]
            acc = v if acc is None else acc + v
        lo = jnp.clip(t - w // 2, 0, n)
        hi = jnp.clip(t - w // 2 + w, 0, n)
        return acc / (hi - lo).astype(F32)

    first = lane < POOL_GROUP_DIM
    m01 = jnp.where(first, window_mean(0, POOL_WINDOWS[0]), window_mean(0, POOL_WINDOWS[1]))
    m23 = jnp.where(first, window_mean(1, POOL_WINDOWS[2]), window_mean(1, POOL_WINDOWS[3]))
    y = (jnp.concatenate([m01, m23], axis=1) - u_ref[0]).astype(BF16)
    o_ref[0] = (_dot(y, w_ref[...]) * sc_ref[...]).astype(BF16)


def _pool(u, w_bd, scale, tp):
    B, n, W = u.shape
    hb = tp // POOL_HALO
    nh = n // POOL_HALO
    return pl.pallas_call(
        functools.partial(_pool_kernel, n, tp),
        out_shape=jax.ShapeDtypeStruct((B, n, W), BF16),
        grid=(B, n // tp),
        in_specs=[pl.BlockSpec((1, tp, W), lambda b, i: (b, i, 0)),
                  pl.BlockSpec((1, POOL_HALO, W), lambda b, i: (b, jnp.maximum(i * hb - 1, 0), 0)),
                  pl.BlockSpec((1, POOL_HALO, W), lambda b, i: (b, jnp.minimum((i + 1) * hb, nh - 1), 0)),
                  _resident(w_bd.shape), _resident(scale.shape)],
        out_specs=pl.BlockSpec((1, tp, W), lambda b, i: (b, i, 0)),
        scratch_shapes=[pltpu.VMEM((tp + 2 * POOL_HALO, W), F32)],
        compiler_params=_params(("arbitrary", "arbitrary")),
        name="pool",
    )(u, u, u, w_bd, scale)


def _head_mask(shape, hd):
    lane = lax.broadcasted_iota(jnp.int32, shape, 1)
    return (lane >= hd * HEAD_DIM) & (lane < (hd + 1) * HEAD_DIM)


def _attend_heads(q, key_parts, val_parts, fix_parts, sink_ref):
    out = jnp.zeros(q.shape, F32)
    for hd in range(N_HEADS):
        qh = jnp.where(_head_mask(q.shape, hd), q, jnp.zeros_like(q))
        scores = []
        for kp, fix in zip(key_parts, fix_parts):
            s = _dot_nt(qh, kp)
            scores.append(s if fix is None else fix(hd, s))
        m = functools.reduce(jnp.maximum, [jnp.max(s, axis=-1, keepdims=True) for s in scores])
        if sink_ref is not None:
            sink = sink_ref[hd:hd + 1, 0:1]
            m = jnp.maximum(m, sink)
        probs = [jnp.exp(s - m) for s in scores]
        den = functools.reduce(jnp.add, [jnp.sum(p, axis=-1, keepdims=True) for p in probs])
        if sink_ref is not None:
            den = den + jnp.exp(sink - m)
        o = functools.reduce(jnp.add, [_dot(p.astype(BF16), vp) for p, vp in zip(probs, val_parts)])
        out = jnp.where(_head_mask(out.shape, hd), o / den, out)
    return out


def _na_kernel(q_ref, k0_ref, k1_ref, k2_ref, v0_ref, v1_ref, v2_ref, kc_ref, vc_ref, bias_ref, o_ref):
    T = ATT_TILE
    fixes = [(lambda hd, s, j=j: s + bias_ref[0, hd, :, j * T:(j + 1) * T]) for j in range(3)] + [None]
    out = _attend_heads(q_ref[0], [k0_ref[0], k1_ref[0], k2_ref[0], kc_ref[0]],
                        [v0_ref[0], v1_ref[0], v2_ref[0], vc_ref[0]], fixes, None)
    o_ref[0] = out.astype(BF16)


def _neighbour_specs(nt):
    T = ATT_TILE
    return [pl.BlockSpec((1, T, 256), lambda b, i: (b, jnp.maximum(i - 1, 0), 0)),
            pl.BlockSpec((1, T, 256), lambda b, i: (b, i, 0)),
            pl.BlockSpec((1, T, 256), lambda b, i: (b, jnp.minimum(i + 1, nt - 1), 0))]


def _na(q, k, v, kc, vc, bias):
    B, S, W = q.shape
    T = ATT_TILE
    nt = S // T
    tile = pl.BlockSpec((1, T, W), lambda b, i: (b, i, 0))
    ctx = pl.BlockSpec((1,) + kc.shape[1:], lambda b, i: (b, 0, 0))
    variant = lambda b, i: (jnp.where(i == 0, 0, jnp.where(i == nt - 1, 2, 1)), 0, 0, 0)
    return pl.pallas_call(
        _na_kernel,
        out_shape=jax.ShapeDtypeStruct((B, S, W), BF16),
        grid=(B, nt),
        in_specs=[tile] + _neighbour_specs(nt) + _neighbour_specs(nt) + [ctx, ctx,
                  pl.BlockSpec((1,) + bias.shape[1:], variant)],
        out_specs=tile,
        compiler_params=_params(("arbitrary", "arbitrary")),
        name="neighbourhood_attention",
    )(q, k, k, k, v, v, v, kc, vc, bias)


def _swa_kernel(S, q_ref, k0_ref, k1_ref, k2_ref, v0_ref, v1_ref, v2_ref, kc_ref, vc_ref, sink_ref, o_ref):
    T = ATT_TILE
    i = pl.program_id(1)
    qpos = i * T + lax.broadcasted_iota(jnp.int32, (T, T), 0)
    col = lax.broadcasted_iota(jnp.int32, (T, T), 1)

    def band(j):
        kpos = (i - 1 + j) * T + col
        ok = (jnp.abs(kpos - qpos) <= SWA_WINDOW) & (kpos >= 0) & (kpos < S)
        return lambda hd, s: jnp.where(ok, s, NEG_INF)

    fixes = [band(0), band(1), band(2), None]
    out = _attend_heads(q_ref[0], [k0_ref[0], k1_ref[0], k2_ref[0], kc_ref[0]],
                        [v0_ref[0], v1_ref[0], v2_ref[0], vc_ref[0]], fixes, sink_ref)
    o_ref[0] = out.astype(BF16)


def _swa(q, k, v, kc, vc, sink_tab):
    B, S, W = q.shape
    T = ATT_TILE
    nt = S // T
    tile = pl.BlockSpec((1, T, W), lambda b, i: (b, i, 0))
    ctx = pl.BlockSpec((1,) + kc.shape[1:], lambda b, i: (b, 0, 0))
    return pl.pallas_call(
        functools.partial(_swa_kernel, S),
        out_shape=jax.ShapeDtypeStruct((B, S, W), BF16),
        grid=(B, nt),
        in_specs=[tile] + _neighbour_specs(nt) + _neighbour_specs(nt) + [ctx, ctx, _resident(sink_tab.shape)],
        out_specs=tile,
        compiler_params=_params(("arbitrary", "arbitrary")),
        name="windowed_attention",
    )(q, k, k, k, v, v, v, kc, vc, sink_tab)


def _mla_finish(acc, wuvt_ref, tq):
    o = (acc[0:MLA_KV_RANK] / acc[MLA_KV_RANK:MLA_KV_RANK + 1]).astype(BF16)
    outs = [_dot(wuvt_ref[hd], o[:, hd * tq:(hd + 1) * tq]) for hd in range(N_HEADS)]
    return jnp.concatenate(outs, axis=0).T


def _mla_kernel(n_chunks, tk, q_ref, k_ref, vt_ref, kc_ref, vtc_ref, wuvt_ref, o_ref,
                s0_ref, s1_ref, m_ref, acc_ref):
    tq = q_ref.shape[2]
    q = q_ref[0].reshape(N_HEADS * tq, q_ref.shape[3])

    def scores(c):
        off = pl.multiple_of(c * tk, tk)
        return _dot_nt(k_ref[0, pl.ds(off, tk), :], q)

    def consume(s, c):
        off = pl.multiple_of(c * tk, tk)
        m_old = m_ref[...]
        m_new = jnp.maximum(m_old, jnp.max(s, axis=0, keepdims=True))
        alpha = jnp.exp2(m_old - m_new)
        p = jnp.exp2(s - m_new).astype(BF16)
        m_ref[...] = m_new
        acc_ref[...] = alpha * acc_ref[...] + _dot(vt_ref[0, :, pl.ds(off, tk)], p)

    s0_ref[...] = scores(0)
    s = _dot_nt(kc_ref[0], q)
    m = jnp.max(s, axis=0, keepdims=True)
    m_ref[...] = m
    acc_ref[...] = _dot(vtc_ref[0], jnp.exp2(s - m).astype(BF16))

    def pair(cc, carry):
        c = 2 * cc
        s1_ref[...] = scores(c + 1)
        consume(s0_ref[...], c)
        s0_ref[...] = scores(c + 2)
        consume(s1_ref[...], c + 1)
        return carry

    lax.fori_loop(0, n_chunks // 2 - 1, pair, 0)
    s1_ref[...] = scores(n_chunks - 1)
    consume(s0_ref[...], n_chunks - 2)
    consume(s1_ref[...], n_chunks - 1)
    o_ref[0] = _mla_finish(acc_ref[...], wuvt_ref, tq).astype(BF16)


def _mla(q, k, vt, kc, vtc, wuvt):
    B, H, S, W = q.shape
    tq, tk = MLA_TQ, MLA_TK
    n_chunks = S // tk
    assert S % tk == 0 and n_chunks % 2 == 0 and n_chunks >= 2
    per_batch = lambda shape: pl.BlockSpec((1,) + shape[1:], lambda b, i: (b,) + (0,) * (len(shape) - 1))
    return pl.pallas_call(
        functools.partial(_mla_kernel, n_chunks, tk),
        out_shape=jax.ShapeDtypeStruct((B, S, BRANCH_W), BF16),
        grid=(B, S // tq),
        in_specs=[pl.BlockSpec((1, H, tq, W), lambda b, i: (b, 0, i, 0)),
                  per_batch(k.shape), per_batch(vt.shape), per_batch(kc.shape), per_batch(vtc.shape),
                  _resident(wuvt.shape)],
        out_specs=pl.BlockSpec((1, tq, BRANCH_W), lambda b, i: (b, i, 0)),
        scratch_shapes=[pltpu.VMEM((tk, H * tq), F32), pltpu.VMEM((tk, H * tq), F32),
                        pltpu.VMEM((1, H * tq), F32), pltpu.VMEM((vt.shape[1], H * tq), F32)],
        compiler_params=_params(("arbitrary", "arbitrary")),
        name="latent_attention",
    )(q, k, vt, kc, vtc, wuvt)


def _ctx_attn_kernel(naq_ref, nak_ref, nav_ref, swq_ref, swk_ref, swv_ref, qmla_ref, kcat_ref, ckvt_ref,
                     sink_ref, wuvt_ref, na_o_ref, sw_o_ref, mla_o_ref):
    na_o_ref[0] = _attend_heads(naq_ref[0], [nak_ref[0]], [nav_ref[0]], [None], None).astype(BF16)
    sw_o_ref[0] = _attend_heads(swq_ref[0], [swk_ref[0]], [swv_ref[0]], [None], sink_ref).astype(BF16)
    tq = qmla_ref.shape[2]
    q = qmla_ref[0].reshape(N_HEADS * tq, qmla_ref.shape[3])
    s = _dot_nt(kcat_ref[0], q)
    p = jnp.exp2(s - jnp.max(s, axis=0, keepdims=True))
    acc = _dot(ckvt_ref[0], p.astype(BF16))
    mla_o_ref[0] = _mla_finish(acc, wuvt_ref, tq).astype(BF16)


def _ctx_attn(naq, nak, nav, swq, swk, swv, qmla, kcat, ckvt, sink_tab, wuvt):
    B, Lc, W = naq.shape
    per_batch = lambda a: pl.BlockSpec((1,) + a.shape[1:], lambda b: (b,) + (0,) * (a.ndim - 1))
    acts = [naq, nak, nav, swq, swk, swv, qmla, kcat, ckvt]
    out = jax.ShapeDtypeStruct((B, Lc, W), BF16)
    return pl.pallas_call(
        _ctx_attn_kernel,
        out_shape=[out, out, out],
        grid=(B,),
        in_specs=[per_batch(a) for a in acts] + [_resident(sink_tab.shape), _resident(wuvt.shape)],
        out_specs=[per_batch(naq)] * 3,
        compiler_params=_params(("arbitrary",)),
        name="context_attention",
    )(*acts, sink_tab, wuvt)


def _merge_kernel(x_ref, mod_ref, g_ref, y0_ref, y1_ref, y2_ref, y3_ref, wg_ref, wb_ref, wo_ref, o_ref):
    x = x_ref[0]
    D = x.shape[1]
    h = _modulated_norm(x, g_ref[...], mod_ref[0, 0:1, :], mod_ref[0, 1:2, :]).astype(BF16)
    merged = None
    for i, y_ref in enumerate((y0_ref, y1_ref, y2_ref, y3_ref)):
        gate = jax.nn.sigmoid(_dot(h, wg_ref[:, i * D:(i + 1) * D]))
        term = gate * _dot(y_ref[0], wb_ref[i])
        merged = term if merged is None else merged + term
    o_ref[0] = x + mod_ref[0, 2:3, :] * _dot(merged.astype(BF16), wo_ref[...])


def _mod_spec(mod, D):
    return pl.BlockSpec((1, 6, D), (lambda b, i: (b, 0, 0)) if mod.shape[0] > 1 else (lambda b, i: (0, 0, 0)))


def _merge(x, mod, g, ys, wg, wb, wo, tm):
    B, n, D = x.shape
    tok = lambda w: pl.BlockSpec((1, tm, w), lambda b, i: (b, i, 0))
    return pl.pallas_call(
        _merge_kernel,
        out_shape=jax.ShapeDtypeStruct((B, n, D), F32),
        grid=(B, n // tm),
        in_specs=[tok(D), _mod_spec(mod, D), _resident(g.shape)] + [tok(BRANCH_W)] * 4
                 + [_resident(wg.shape), _resident(wb.shape), _resident(wo.shape)],
        out_specs=tok(D),
        compiler_params=_params(("arbitrary", "arbitrary")),
        name="merge",
    )(x, mod, g, *ys, wg, wb, wo)


def _ffn_kernel(final, x_ref, mod_ref, g_ref, w1_ref, w3_ref, w2_ref, *rest):
    if final:
        fg_ref, o_ref = rest
    else:
        (o_ref,) = rest
    x = x_ref[0]
    h = _modulated_norm(x, g_ref[...], mod_ref[0, 3:4, :], mod_ref[0, 4:5, :]).astype(BF16)
    a = _dot(h, w1_ref[...])
    act = ((a * jax.nn.sigmoid(a)) * _dot(h, w3_ref[...])).astype(BF16)
    y = x + mod_ref[0, 5:6, :] * _dot(act, w2_ref[...])
    o_ref[0] = _rms(y, fg_ref[...]) if final else y


def _ffn(x, mod, g, w1, w3, w2, final_g, tm):
    B, n, D = x.shape
    final = final_g is not None
    tok = pl.BlockSpec((1, tm, D), lambda b, i: (b, i, 0))
    in_specs = [tok, _mod_spec(mod, D), _resident(g.shape), _resident(w1.shape), _resident(w3.shape),
                _resident(w2.shape)]
    args = [x, mod, g, w1, w3, w2]
    if final:
        in_specs.append(_resident(final_g.shape))
        args.append(final_g)
    return pl.pallas_call(
        functools.partial(_ffn_kernel, final),
        out_shape=jax.ShapeDtypeStruct((B, n, D), F32),
        grid=(B, n // tm),
        in_specs=in_specs, out_specs=tok,
        compiler_params=_params(("arbitrary", "arbitrary")),
        name="ffn_final" if final else "ffn",
    )(*args)


def _pack_w_in(w):
    def dup_heads(a):
        return jnp.concatenate([a[:, :64], a[:, :64], a[:, 64:], a[:, 64:]], axis=1)

    kr = w[:, OFF_MLA_KR:KV_COLS]
    cols = [w[:, OFF_NA_Q:OFF_NA_Q + 256], w[:, OFF_NA_K:OFF_NA_K + 256], w[:, OFF_NA_V:OFF_NA_V + 256],
            w[:, OFF_SWA_Q:OFF_SWA_Q + 256], dup_heads(w[:, OFF_SWA_K:OFF_SWA_V]),
            dup_heads(w[:, OFF_SWA_V:OFF_MLA_CKV]), w[:, OFF_MLA_CKV:OFF_MLA_KR],
            jnp.concatenate([kr] * N_HEADS, axis=1), w[:, OFF_MLA_CQ:OFF_POOL], w[:, OFF_POOL:OFF_GATE]]
    return jnp.concatenate(cols, axis=1).astype(BF16)


def _block_diag(blocks):
    r, c = blocks[0].shape
    n = len(blocks)
    rows = [jnp.concatenate([blocks[i] if j == i else jnp.zeros((r, c), blocks[0].dtype) for j in range(n)], axis=1)
            for i in range(n)]
    return jnp.concatenate(rows, axis=0)


def _rope_tables(n, dim):
    t = jnp.arange(n, dtype=jnp.int32)
    row = (t // GRID_W).astype(F32)
    col = (t % GRID_W).astype(F32)
    n_freq = dim // 4
    inv = jnp.power(ROPE_BASE, -jnp.arange(n_freq, dtype=F32) / n_freq)
    ang = jnp.concatenate([row[:, None] * inv, col[:, None] * inv], axis=-1)
    cos, sin = jnp.cos(ang), jnp.sin(ang)
    reps = LANES // dim
    cos_t = jnp.tile(jnp.concatenate([cos, cos], axis=1), (1, reps))
    sin_t = jnp.tile(jnp.concatenate([-sin, sin], axis=1), (1, reps))
    return cos_t, sin_t


def _na_bias_tables(rpb, rows):
    T = ATT_TILE
    tr = T // GRID_W
    nt = rows // tr
    H = rpb.shape[0]
    pad = GRID_W - NA_COLS
    padded = jnp.pad(rpb.astype(F32), ((0, 0), (0, 0), (pad, pad)))
    toeplitz = jnp.stack([padded[:, :, GRID_W - 1 - qc:2 * GRID_W - 1 - qc] for qc in range(GRID_W)], axis=2)
    qc, kc = np.arange(GRID_W)[:, None], np.arange(GRID_W)[None, :]
    c0 = np.clip(qc - NA_COLS // 2, 0, GRID_W - NA_COLS)
    col_ok = (kc >= c0) & (kc < c0 + NA_COLS)
    blocks = jnp.where(jnp.asarray(col_ok)[None, None], toeplitz, NEG_INF)
    masked = jnp.full((H, GRID_W, GRID_W), NEG_INF, F32)
    tabs = []
    for i in (0, 1, nt - 1):
        rows_out = []
        for ql in range(tr):
            qr = tr * i + ql
            r0 = min(max(qr - NA_ROWS // 2, 0), rows - NA_ROWS)
            row = []
            for kl in range(3 * tr):
                kr = tr * (i - 1) + kl
                row.append(blocks[:, kr - qr + NA_ROWS - 1] if r0 <= kr < r0 + NA_ROWS else masked)
            rows_out.append(jnp.concatenate(row, axis=2))
        tabs.append(jnp.concatenate(rows_out, axis=1))
    return jnp.stack(tabs, axis=0)


def kernel(x, c, ctx, c_ctx, ada_w, ada_b, norm1_g, norm2_g, w_in, pool_w, pool_scale, na_rpb, swa_sink,
           mla_q_norm, mla_kv_norm, mla_w_uq, mla_w_uk, mla_w_uv, w_branch, w_out, ffn_w1, ffn_w3, ffn_w2,
           final_norm_g):
    B, S, D = x.shape
    Lc = ctx.shape[1]
    depth = ada_w.shape[0]
    rows = S // GRID_W
    assert S % 1024 == 0 and rows >= 3 * (ATT_TILE // GRID_W) and Lc == ATT_TILE and B <= 7
    tm_x, tm_c = 512, Lc

    cond = jnp.concatenate([c, c_ctx[None], jnp.zeros((8 - B - 1, D), F32)], axis=0)
    mods = _modulation(cond, ada_w.astype(BF16), ada_b[:, None, :])
    rope_tabs = _rope_tables(S, HEAD_DIM) + _rope_tables(S, MLA_ROPE)
    row2 = lambda v: v.reshape(1, -1)

    xc = ctx
    for l in range(depth):
        last = l == depth - 1
        mod_x = mods[l, :B].reshape(B, 6, D)
        mod_c = mods[l, B:B + 1].reshape(1, 6, D)
        wp = _pack_w_in(w_in[l])
        wg = w_in[l][:, OFF_GATE:].astype(BF16)
        uq = mla_w_uq[l].reshape(-1, N_HEADS, MLA_NOPE + MLA_ROPE)
        wuq = jnp.concatenate([uq[:, :, :MLA_NOPE].reshape(-1, N_HEADS * MLA_NOPE),
                               uq[:, :, MLA_NOPE:].reshape(-1, N_HEADS * MLA_ROPE)], axis=1).astype(BF16)
        uk = jnp.transpose(mla_w_uk[l], (1, 2, 0))
        wuk = _block_diag([uk[h] for h in range(N_HEADS)]).astype(BF16)
        wuvt = jnp.transpose(mla_w_uv[l], (1, 2, 0)).astype(BF16)
        w_pool = _block_diag([pool_w[l, g] for g in range(len(POOL_WINDOWS))]).astype(BF16)
        sink_tab = jnp.concatenate([jnp.broadcast_to(swa_sink[l][:, None], (N_HEADS, LANES)),
                                    jnp.zeros((8 - N_HEADS, LANES), F32)], axis=0)
        bias = _na_bias_tables(na_rpb[l], rows)
        g1, g2 = row2(norm1_g[l]), row2(norm2_g[l])
        qg, kvg, psc = row2(mla_q_norm[l]), row2(mla_kv_norm[l]), row2(pool_scale[l])
        wb, wo = w_branch[l].astype(BF16), w_out[l].astype(BF16)
        w1, w3, w2 = ffn_w1[l].astype(BF16), ffn_w3[l].astype(BF16), ffn_w2[l].astype(BF16)

        (naq, nak, nav, swq, swk, swv, kcat, ckvt, qmla, pool_in) = _inproj(
            x, mod_x, g1, wp, wuq, wuk, qg, kvg, rope_tabs, tm_x)
        (c_naq, c_nak, c_nav, c_swq, c_swk, c_swv, c_kcat, c_ckvt, c_qmla, c_pool_in) = _inproj(
            xc, mod_c, g1, wp, wuq, wuk, qg, kvg, None, tm_c)

        ys = [_pool(pool_in, w_pool, psc, 512),
              _na(naq, nak, nav, c_nak, c_nav, bias),
              _swa(swq, swk, swv, c_swk, c_swv, sink_tab),
              _mla(qmla, kcat, ckvt, c_kcat, c_ckvt, wuvt)]
        if not last:
            ys_c = [_pool(c_pool_in, w_pool, psc, Lc)] + list(_ctx_attn(
                c_naq, c_nak, c_nav, c_swq, c_swk, c_swv, c_qmla, c_kcat, c_ckvt, sink_tab, wuvt))
            xc = _merge(xc, mod_c, g1, ys_c, wg, wb, wo, tm_c)
            xc = _ffn(xc, mod_c, g2, w1, w3, w2, None, tm_c)
        x = _merge(x, mod_x, g1, ys, wg, wb, wo, tm_x)
        x = _ffn(x, mod_x, g2, w1, w3, w2, row2(final_norm_g) if last else None, tm_x)
    return x
```

```python
import functools

import numpy as np
import jax
import jax.numpy as jnp
from jax import lax
from jax.experimental import pallas as pl
from jax.experimental.pallas import tpu as pltpu

F32 = jnp.float32
BF16 = jnp.bfloat16

GRID_W = 64
HEAD_DIM = 64
ROPE_BASE = 10000.0
NEG_INF = -1e30
EPS = 1e-6
POOL_WINDOWS = (2, 4, 8, 16)
POOL_GROUP_DIM = 64
N_HEADS = 4
NA_ROWS = 8
NA_COLS = 16
SWA_WINDOW = 128
MLA_KV_RANK = 128
MLA_NOPE = 64
MLA_ROPE = 32
MLA_SCALE = (MLA_NOPE + MLA_ROPE) ** -0.5
LOG2_E = 1.4426950408889634
MLA_Q_SCALE = MLA_SCALE * LOG2_E
MLA_SUM_ROWS = 16
MLA_LAG_LIMIT = 64.0
ATT_SCALE = HEAD_DIM ** -0.5
BRANCH_W = 256
N_BRANCH = 4

OFF_NA_K, OFF_NA_V, OFF_SWA_K, OFF_SWA_V, OFF_MLA_CKV, OFF_MLA_KR = 0, 256, 512, 640, 768, 896
KV_COLS = 928
OFF_NA_Q, OFF_SWA_Q, OFF_MLA_CQ, OFF_POOL, OFF_GATE = 928, 1184, 1440, 1696, 1952

LANES = 128
ATT_TILE = 256
MLA_TQ = 256
MLA_TK = 512
MLA_UNROLL = 8
POOL_HALO = 8
VMEM_LIMIT = 56 * 1024 * 1024


def _params(sem):
    return pltpu.CompilerParams(dimension_semantics=sem, vmem_limit_bytes=VMEM_LIMIT)


def _dot(a, b):
    return jnp.dot(a, b, preferred_element_type=F32)


def _dot_nt(a, b):
    return lax.dot_general(a, b, (((1,), (1,)), ((), ())), preferred_element_type=F32)


def _resident(shape):
    n = len(shape)
    return pl.BlockSpec(shape, lambda *_: (0,) * n, pipeline_mode=pl.Buffered(1))


def _modulated_norm(x, g, shift, scale):
    y = x * lax.rsqrt(jnp.mean(x * x, axis=-1, keepdims=True) + EPS)
    return (y * g) * (1.0 + scale) + shift


def _rms(x, g):
    return x * lax.rsqrt(jnp.mean(x * x, axis=-1, keepdims=True) + EPS) * g


def _rope_lanes(t, cos, sin_signed, half):
    lane = lax.broadcasted_iota(jnp.int32, t.shape, 1)
    up = pltpu.roll(t, LANES - half, 1)
    down = pltpu.roll(t, half, 1)
    partner = jnp.where((lane & (2 * half - 1)) < half, up, down)
    return t * cos + partner * sin_signed


def _mod_kernel(c_ref, w_ref, b_ref, o_ref):
    c = c_ref[...]
    s = (c * jax.nn.sigmoid(c)).astype(BF16)
    o_ref[0] = _dot(s, w_ref[0]) + b_ref[0]


def _modulation(cond, ada_w, ada_b):
    L, D, N = ada_w.shape
    tn = N // 4
    return pl.pallas_call(
        _mod_kernel,
        out_shape=jax.ShapeDtypeStruct((L, cond.shape[0], N), F32),
        grid=(L, N // tn),
        in_specs=[pl.BlockSpec(cond.shape, lambda l, j: (0, 0)),
                  pl.BlockSpec((1, D, tn), lambda l, j: (l, 0, j)),
                  pl.BlockSpec((1, 1, tn), lambda l, j: (l, 0, j))],
        out_specs=pl.BlockSpec((1, cond.shape[0], tn), lambda l, j: (l, 0, j)),
        compiler_params=_params(("arbitrary", "arbitrary")),
        name="modulation",
    )(cond, ada_w, ada_b)


def _inproj_kernel(rope, x_ref, mod_ref, g_ref, w_ref, wuq_ref, wuk_ref, qg_ref, kvg_ref, *rest):
    if rope:
        c64_ref, s64_ref, c32_ref, s32_ref = rest[:4]
        rest = rest[4:]
    (naq_ref, nak_ref, nav_ref, swq_ref, swk_ref, swv_ref, kcat_ref, ckvt_ref, qmla_ref, pool_ref,
     swvt_ref) = rest

    x = x_ref[0]
    h = _modulated_norm(x, g_ref[...], mod_ref[0, 0:1, :], mod_ref[0, 1:2, :]).astype(BF16)

    def seg(a, n):
        return _dot(h, w_ref[:, a:a + n])

    def rope64(t):
        if not rope:
            return t
        c, s = c64_ref[...], s64_ref[...]
        return jnp.concatenate([_rope_lanes(t[:, :LANES], c, s, 32), _rope_lanes(t[:, LANES:], c, s, 32)], axis=1)

    def rope32(t):
        return _rope_lanes(t, c32_ref[...], s32_ref[...], 16) if rope else t

    naq_ref[0] = (seg(0, 256) * ATT_SCALE).astype(BF16)
    nak_ref[0] = seg(256, 256).astype(BF16)
    nav_ref[0] = seg(512, 256).astype(BF16)
    swq_ref[0] = (rope64(seg(768, 256)) * ATT_SCALE).astype(BF16)
    swk_ref[0] = rope64(seg(1024, 256)).astype(BF16)
    swv = seg(1280, 256)
    swv_ref[0] = swv.astype(BF16)
    swvt_ref[0] = swv.T.astype(BF16)

    ckv = _rms(seg(1536, 128), kvg_ref[...])
    kcat_ref[0, :, 0:LANES] = ckv.astype(BF16)
    kcat_ref[0, :, LANES:2 * LANES] = rope32(seg(1664, 128)).astype(BF16)
    ckvt_ref[0, 0:MLA_KV_RANK, :] = ckv.T.astype(BF16)
    ckvt_ref[0, MLA_KV_RANK:, :] = jnp.ones((MLA_SUM_ROWS, ckv.shape[0]), BF16)

    cq = _rms(seg(1792, 256), qg_ref[...]).astype(BF16)
    qq = _dot(cq, wuq_ref[...])
    q_lat = _dot(qq[:, :256].astype(BF16), wuk_ref[...])
    q_rope = rope32(qq[:, 256:384])
    lane = lax.broadcasted_iota(jnp.int32, q_rope.shape, 1)
    for hd in range(N_HEADS):
        qmla_ref[0, hd, :, 0:LANES] = (q_lat[:, hd * LANES:(hd + 1) * LANES] * MLA_Q_SCALE).astype(BF16)
        own = (lane >= hd * MLA_ROPE) & (lane < (hd + 1) * MLA_ROPE)
        qmla_ref[0, hd, :, LANES:2 * LANES] = (jnp.where(own, q_rope, 0.0) * MLA_Q_SCALE).astype(BF16)

    pool_ref[0] = seg(2048, 256)


def _inproj(x, mod, g, wp, wuq, wuk, qg, kvg, rope_tabs, tm):
    B, n, D = x.shape
    rope = rope_tabs is not None
    mod_b = mod.shape[0]
    tok = lambda w: pl.BlockSpec((1, tm, w), lambda b, i: (b, i, 0))
    in_specs = [tok(D),
                pl.BlockSpec((1, 6, D), (lambda b, i: (b, 0, 0)) if mod_b > 1 else (lambda b, i: (0, 0, 0))),
                _resident(g.shape), _resident(wp.shape), _resident(wuq.shape), _resident(wuk.shape),
                _resident(qg.shape), _resident(kvg.shape)]
    args = [x, mod, g, wp, wuq, wuk, qg, kvg]
    if rope:
        in_specs += [pl.BlockSpec((tm, LANES), lambda b, i: (i, 0))] * 4
        args += list(rope_tabs)
    bf = lambda w: jax.ShapeDtypeStruct((B, n, w), BF16)
    vt_rows = MLA_KV_RANK + MLA_SUM_ROWS
    out_shape = [bf(256)] * 7 + [jax.ShapeDtypeStruct((B, vt_rows, n), BF16),
                                 jax.ShapeDtypeStruct((B, N_HEADS, n, 256), BF16),
                                 jax.ShapeDtypeStruct((B, n, 256), F32), jax.ShapeDtypeStruct((B, 256, n), BF16)]
    out_specs = [tok(256)] * 7 + [pl.BlockSpec((1, vt_rows, tm), lambda b, i: (b, 0, i)),
                                  pl.BlockSpec((1, N_HEADS, tm, 256), lambda b, i: (b, 0, i, 0)),
                                  tok(256), pl.BlockSpec((1, 256, tm), lambda b, i: (b, 0, i))]
    return pl.pallas_call(
        functools.partial(_inproj_kernel, rope),
        out_shape=out_shape, grid=(B, n // tm), in_specs=in_specs, out_specs=out_specs,
        compiler_params=_params(("arbitrary", "arbitrary")),
        name="inproj_rope" if rope else "inproj_ctx",
    )(*args)


def _pool_kernel(n, tp, u_ref, prev_ref, next_ref, w_ref, sc_ref, o_ref, ext_ref):
    i = pl.program_id(1)
    last = pl.num_programs(1) - 1
    ext_ref[0:POOL_HALO, :] = jnp.where(i > 0, prev_ref[0], 0.0)
    ext_ref[POOL_HALO:POOL_HALO + tp, :] = u_ref[0]
    ext_ref[POOL_HALO + tp:, :] = jnp.where(i < last, next_ref[0], 0.0)

    t = i * tp + lax.broadcasted_iota(jnp.int32, (tp, LANES), 0)
    lane = lax.broadcasted_iota(jnp.int32, (tp, LANES), 1)

    def window_mean(col, w):
        acc = None
        for k in range(-(w // 2), w - w // 2):
            v = ext_ref[POOL_HALO + k:POOL_HALO + k + tp, col * LANES:(col + 1) * LANES]
            acc = v if acc is None else acc + v
        lo = jnp.clip(t - w // 2, 0, n)
        hi = jnp.clip(t - w // 2 + w, 0, n)
        return acc / (hi - lo).astype(F32)

    first = lane < POOL_GROUP_DIM
    m01 = jnp.where(first, window_mean(0, POOL_WINDOWS[0]), window_mean(0, POOL_WINDOWS[1]))
    m23 = jnp.where(first, window_mean(1, POOL_WINDOWS[2]), window_mean(1, POOL_WINDOWS[3]))
    y = (jnp.concatenate([m01, m23], axis=1) - u_ref[0]).astype(BF16)
    o_ref[0] = (_dot(y, w_ref[...]) * sc_ref[...]).astype(BF16)


def _pool(u, w_bd, scale, tp):
    B, n, W = u.shape
    hb = tp // POOL_HALO
    nh = n // POOL_HALO
    return pl.pallas_call(
        functools.partial(_pool_kernel, n, tp),
        out_shape=jax.ShapeDtypeStruct((B, n, W), BF16),
        grid=(B, n // tp),
        in_specs=[pl.BlockSpec((1, tp, W), lambda b, i: (b, i, 0)),
                  pl.BlockSpec((1, POOL_HALO, W), lambda b, i: (b, jnp.maximum(i * hb - 1, 0), 0)),
                  pl.BlockSpec((1, POOL_HALO, W), lambda b, i: (b, jnp.minimum((i + 1) * hb, nh - 1), 0)),
                  _resident(w_bd.shape), _resident(scale.shape)],
        out_specs=pl.BlockSpec((1, tp, W), lambda b, i: (b, i, 0)),
        scratch_shapes=[pltpu.VMEM((tp + 2 * POOL_HALO, W), F32)],
        compiler_params=_params(("arbitrary", "arbitrary")),
        name="pool",
    )(u, u, u, w_bd, scale)


def _head_mask(shape, hd):
    lane = lax.broadcasted_iota(jnp.int32, shape, 1)
    return (lane >= hd * HEAD_DIM) & (lane < (hd + 1) * HEAD_DIM)


def _attend_heads(q, key_parts, val_parts, fix_parts, sink_ref):
    out = jnp.zeros(q.shape, F32)
    for hd in range(N_HEADS):
        qh = jnp.where(_head_mask(q.shape, hd), q, jnp.zeros_like(q))
        scores = []
        for kp, fix in zip(key_parts, fix_parts):
            s = _dot_nt(qh, kp)
            scores.append(s if fix is None else fix(hd, s))
        m = functools.reduce(jnp.maximum, [jnp.max(s, axis=-1, keepdims=True) for s in scores])
        if sink_ref is not None:
            sink = sink_ref[hd:hd + 1, 0:1]
            m = jnp.maximum(m, sink)
        probs = [jnp.exp(s - m) for s in scores]
        den = functools.reduce(jnp.add, [jnp.sum(p, axis=-1, keepdims=True) for p in probs])
        if sink_ref is not None:
            den = den + jnp.exp(sink - m)
        o = functools.reduce(jnp.add, [_dot(p.astype(BF16), vp) for p, vp in zip(probs, val_parts)])
        out = jnp.where(_head_mask(out.shape, hd), o / den, out)
    return out


def _attend_heads_t(q, key_parts, valt_parts, fix_parts, sink_ref):
    outs = []
    for hd in range(N_HEADS):
        qh = jnp.where(_head_mask(q.shape, hd), q, jnp.zeros_like(q))
        scores = []
        for kp, fix in zip(key_parts, fix_parts):
            s = _dot_nt(kp, qh)
            scores.append(s if fix is None else fix(hd, s))
        m = functools.reduce(jnp.maximum, [jnp.max(s, axis=0, keepdims=True) for s in scores])
        if sink_ref is not None:
            sink = sink_ref[hd:hd + 1, :]
            m = jnp.maximum(m, sink)
        probs = [jnp.exp(s - m) for s in scores]
        den = functools.reduce(jnp.add, [jnp.sum(p, axis=0, keepdims=True) for p in probs])
        if sink_ref is not None:
            den = den + jnp.exp(sink - m)
        o = functools.reduce(jnp.add, [_dot(vt[hd * HEAD_DIM:(hd + 1) * HEAD_DIM, :], p.astype(BF16))
                                       for p, vt in zip(probs, valt_parts)])
        outs.append(o / den)
    return jnp.concatenate(outs, axis=0).T


def _na_kernel(q_ref, k0_ref, k1_ref, k2_ref, v0_ref, v1_ref, v2_ref, kc_ref, vc_ref, bias_ref, o_ref):
    T = ATT_TILE
    fixes = [(lambda hd, s, j=j: s + bias_ref[0, hd, :, j * T:(j + 1) * T]) for j in range(3)] + [None]
    out = _attend_heads(q_ref[0], [k0_ref[0], k1_ref[0], k2_ref[0], kc_ref[0]],
                        [v0_ref[0], v1_ref[0], v2_ref[0], vc_ref[0]], fixes, None)
    o_ref[0] = out.astype(BF16)


def _neighbour_specs(nt, transposed):
    T = ATT_TILE
    lo, mid, hi = (lambda i: jnp.maximum(i - 1, 0)), (lambda i: i), (lambda i: jnp.minimum(i + 1, nt - 1))
    if transposed:
        return [pl.BlockSpec((1, 256, T), lambda b, i, f=f: (b, 0, f(i))) for f in (lo, mid, hi)]
    return [pl.BlockSpec((1, T, 256), lambda b, i, f=f: (b, f(i), 0)) for f in (lo, mid, hi)]


def _na(q, k, v, kc, vc, bias):
    B, S, W = q.shape
    T = ATT_TILE
    nt = S // T
    tile = pl.BlockSpec((1, T, W), lambda b, i: (b, i, 0))
    ctx = lambda a: pl.BlockSpec((1,) + a.shape[1:], lambda b, i: (b, 0, 0))
    variant = lambda b, i: (jnp.where(i == 0, 0, jnp.where(i == nt - 1, 2, 1)), 0, 0, 0)
    return pl.pallas_call(
        _na_kernel,
        out_shape=jax.ShapeDtypeStruct((B, S, W), BF16),
        grid=(B, nt),
        in_specs=[tile] + _neighbour_specs(nt, False) + _neighbour_specs(nt, False) + [ctx(kc), ctx(vc),
                  pl.BlockSpec((1,) + bias.shape[1:], variant)],
        out_specs=tile,
        compiler_params=_params(("arbitrary", "arbitrary")),
        name="neighbourhood_attention",
    )(q, k, k, k, v, v, v, kc, vc, bias)


def _swa_kernel(S, q_ref, k0_ref, k1_ref, k2_ref, v0_ref, v1_ref, v2_ref, kc_ref, vc_ref, sink_ref, o_ref):
    T = ATT_TILE
    i = pl.program_id(1)
    row = lax.broadcasted_iota(jnp.int32, (T, T), 0)
    qpos = i * T + lax.broadcasted_iota(jnp.int32, (T, T), 1)

    def band(j):
        kpos = (i - 1 + j) * T + row
        ok = (jnp.abs(kpos - qpos) <= SWA_WINDOW) & (kpos >= 0) & (kpos < S)
        return lambda hd, s: jnp.where(ok, s, NEG_INF)

    fixes = [band(0), band(1), band(2), None]
    out = _attend_heads_t(q_ref[0], [k0_ref[0], k1_ref[0], k2_ref[0], kc_ref[0]],
                          [v0_ref[0], v1_ref[0], v2_ref[0], vc_ref[0]], fixes, sink_ref)
    o_ref[0] = out.astype(BF16)


def _swa(q, k, vt, kc, vtc, sink_tab):
    B, S, W = q.shape
    T = ATT_TILE
    nt = S // T
    tile = pl.BlockSpec((1, T, W), lambda b, i: (b, i, 0))
    ctx = lambda a: pl.BlockSpec((1,) + a.shape[1:], lambda b, i: (b, 0, 0))
    return pl.pallas_call(
        functools.partial(_swa_kernel, S),
        out_shape=jax.ShapeDtypeStruct((B, S, W), BF16),
        grid=(B, nt),
        in_specs=[tile] + _neighbour_specs(nt, False) + _neighbour_specs(nt, True)
                 + [ctx(kc), ctx(vtc), _resident(sink_tab.shape)],
        out_specs=tile,
        compiler_params=_params(("arbitrary", "arbitrary")),
        name="windowed_attention",
    )(q, k, k, k, vt, vt, vt, kc, vtc, sink_tab)


def _mla_finish(acc, wuvt_ref, tq):
    o = (acc[0:MLA_KV_RANK] / acc[MLA_KV_RANK:MLA_KV_RANK + 1]).astype(BF16)
    outs = [_dot(wuvt_ref[hd], o[:, hd * tq:(hd + 1) * tq]) for hd in range(N_HEADS)]
    return jnp.concatenate(outs, axis=0).T


def _mla_kernel(n_chunks, unroll, tk, q_ref, k_ref, vt_ref, kc_ref, vtc_ref, wuvt_ref, o_ref,
                m_ref, ex_ref, acc_ref):
    tq = q_ref.shape[2]
    q = q_ref[0].reshape(N_HEADS * tq, q_ref.shape[3])

    def scores(c):
        off = pl.multiple_of(c * tk, tk)
        return _dot_nt(k_ref[0, pl.ds(off, tk), :], q)

    def values(c):
        return vt_ref[0, :, pl.ds(pl.multiple_of(c * tk, tk), tk)]

    def start():
        s = _dot_nt(kc_ref[0], q)
        m = jnp.max(s, axis=0, keepdims=True)
        m_ref[...] = m
        acc_ref[...] = _dot(vtc_ref[0], jnp.exp2(s - m).astype(BF16))

    def lagged(c):
        s = scores(c)
        m_old = m_ref[...]
        p = jnp.exp2(s - m_old).astype(BF16)
        cm = jnp.max(s, axis=0, keepdims=True)
        m_new = jnp.maximum(m_old, cm)
        ex_ref[...] = jnp.maximum(ex_ref[...], cm - m_old)
        m_ref[...] = m_new
        acc_ref[...] = (acc_ref[...] + _dot(values(c), p)) * jnp.exp2(m_old - m_new)

    def exact(c, carry):
        s = scores(c)
        m_old = m_ref[...]
        m_new = jnp.maximum(m_old, jnp.max(s, axis=0, keepdims=True))
        p = jnp.exp2(s - m_new).astype(BF16)
        m_ref[...] = m_new
        acc_ref[...] = jnp.exp2(m_old - m_new) * acc_ref[...] + _dot(values(c), p)
        return carry

    def group(cc, carry):
        for j in range(unroll):
            lagged(unroll * cc + j)
        return carry

    start()
    ex_ref[...] = jnp.zeros(ex_ref.shape, F32)
    lax.fori_loop(0, n_chunks // unroll, group, 0)

    @pl.when(jnp.max(ex_ref[...]) > MLA_LAG_LIMIT)
    def _():
        start()
        lax.fori_loop(0, n_chunks, exact, 0)

    o_ref[0] = _mla_finish(acc_ref[...], wuvt_ref, tq).astype(BF16)


def _mla(q, k, vt, kc, vtc, wuvt):
    B, H, S, W = q.shape
    tq, tk = MLA_TQ, MLA_TK
    n_chunks = S // tk
    assert S % tk == 0
    unroll = max(u for u in range(1, MLA_UNROLL + 1) if n_chunks % u == 0)
    per_batch = lambda shape: pl.BlockSpec((1,) + shape[1:], lambda b, i: (b,) + (0,) * (len(shape) - 1))
    return pl.pallas_call(
        functools.partial(_mla_kernel, n_chunks, unroll, tk),
        out_shape=jax.ShapeDtypeStruct((B, S, BRANCH_W), BF16),
        grid=(B, S // tq),
        in_specs=[pl.BlockSpec((1, H, tq, W), lambda b, i: (b, 0, i, 0)),
                  per_batch(k.shape), per_batch(vt.shape), per_batch(kc.shape), per_batch(vtc.shape),
                  _resident(wuvt.shape)],
        out_specs=pl.BlockSpec((1, tq, BRANCH_W), lambda b, i: (b, i, 0)),
        scratch_shapes=[pltpu.VMEM((1, H * tq), F32), pltpu.VMEM((1, H * tq), F32),
                        pltpu.VMEM((vt.shape[1], H * tq), F32)],
        compiler_params=_params(("arbitrary", "arbitrary")),
        name="latent_attention",
    )(q, k, vt, kc, vtc, wuvt)


def _ctx_attn_kernel(naq_ref, nak_ref, nav_ref, swq_ref, swk_ref, swv_ref, qmla_ref, kcat_ref, ckvt_ref,
                     sink_ref, wuvt_ref, na_o_ref, sw_o_ref, mla_o_ref):
    na_o_ref[0] = _attend_heads(naq_ref[0], [nak_ref[0]], [nav_ref[0]], [None], None).astype(BF16)
    sw_o_ref[0] = _attend_heads(swq_ref[0], [swk_ref[0]], [swv_ref[0]], [None], sink_ref).astype(BF16)
    tq = qmla_ref.shape[2]
    q = qmla_ref[0].reshape(N_HEADS * tq, qmla_ref.shape[3])
    s = _dot_nt(kcat_ref[0], q)
    p = jnp.exp2(s - jnp.max(s, axis=0, keepdims=True))
    acc = _dot(ckvt_ref[0], p.astype(BF16))
    mla_o_ref[0] = _mla_finish(acc, wuvt_ref, tq).astype(BF16)


def _ctx_attn(naq, nak, nav, swq, swk, swv, qmla, kcat, ckvt, sink_tab, wuvt):
    B, Lc, W = naq.shape
    per_batch = lambda a: pl.BlockSpec((1,) + a.shape[1:], lambda b: (b,) + (0,) * (a.ndim - 1))
    acts = [naq, nak, nav, swq, swk, swv, qmla, kcat, ckvt]
    out = jax.ShapeDtypeStruct((B, Lc, W), BF16)
    return pl.pallas_call(
        _ctx_attn_kernel,
        out_shape=[out, out, out],
        grid=(B,),
        in_specs=[per_batch(a) for a in acts] + [_resident(sink_tab.shape), _resident(wuvt.shape)],
        out_specs=[per_batch(naq)] * 3,
        compiler_params=_params(("arbitrary",)),
        name="context_attention",
    )(*acts, sink_tab, wuvt)


def _merge_kernel(x_ref, mod_ref, g_ref, y0_ref, y1_ref, y2_ref, y3_ref, wg_ref, wb_ref, wo_ref, o_ref):
    x = x_ref[0]
    D = x.shape[1]
    h = _modulated_norm(x, g_ref[...], mod_ref[0, 0:1, :], mod_ref[0, 1:2, :]).astype(BF16)
    merged = None
    for i, y_ref in enumerate((y0_ref, y1_ref, y2_ref, y3_ref)):
        gate = jax.nn.sigmoid(_dot(h, wg_ref[:, i * D:(i + 1) * D]))
        term = gate * _dot(y_ref[0], wb_ref[i])
        merged = term if merged is None else merged + term
    o_ref[0] = x + mod_ref[0, 2:3, :] * _dot(merged.astype(BF16), wo_ref[...])


def _mod_spec(mod, D):
    return pl.BlockSpec((1, 6, D), (lambda b, i: (b, 0, 0)) if mod.shape[0] > 1 else (lambda b, i: (0, 0, 0)))


def _merge(x, mod, g, ys, wg, wb, wo, tm):
    B, n, D = x.shape
    tok = lambda w: pl.BlockSpec((1, tm, w), lambda b, i: (b, i, 0))
    return pl.pallas_call(
        _merge_kernel,
        out_shape=jax.ShapeDtypeStruct((B, n, D), F32),
        grid=(B, n // tm),
        in_specs=[tok(D), _mod_spec(mod, D), _resident(g.shape)] + [tok(BRANCH_W)] * 4
                 + [_resident(wg.shape), _resident(wb.shape), _resident(wo.shape)],
        out_specs=tok(D),
        compiler_params=_params(("arbitrary", "arbitrary")),
        name="merge",
    )(x, mod, g, *ys, wg, wb, wo)


def _ffn_kernel(final, x_ref, mod_ref, g_ref, w1_ref, w3_ref, w2_ref, *rest):
    if final:
        fg_ref, o_ref = rest
    else:
        (o_ref,) = rest
    x = x_ref[0]
    h = _modulated_norm(x, g_ref[...], mod_ref[0, 3:4, :], mod_ref[0, 4:5, :]).astype(BF16)
    a = _dot(h, w1_ref[...])
    act = ((a * jax.nn.sigmoid(a)) * _dot(h, w3_ref[...])).astype(BF16)
    y = x + mod_ref[0, 5:6, :] * _dot(act, w2_ref[...])
    o_ref[0] = _rms(y, fg_ref[...]) if final else y


def _ffn(x, mod, g, w1, w3, w2, final_g, tm):
    B, n, D = x.shape
    final = final_g is not None
    tok = pl.BlockSpec((1, tm, D), lambda b, i: (b, i, 0))
    in_specs = [tok, _mod_spec(mod, D), _resident(g.shape), _resident(w1.shape), _resident(w3.shape),
                _resident(w2.shape)]
    args = [x, mod, g, w1, w3, w2]
    if final:
        in_specs.append(_resident(final_g.shape))
        args.append(final_g)
    return pl.pallas_call(
        functools.partial(_ffn_kernel, final),
        out_shape=jax.ShapeDtypeStruct((B, n, D), F32),
        grid=(B, n // tm),
        in_specs=in_specs, out_specs=tok,
        compiler_params=_params(("arbitrary", "arbitrary")),
        name="ffn_final" if final else "ffn",
    )(*args)


def _pack_w_in(w):
    def dup_heads(a):
        return jnp.concatenate([a[:, :64], a[:, :64], a[:, 64:], a[:, 64:]], axis=1)

    kr = w[:, OFF_MLA_KR:KV_COLS]
    cols = [w[:, OFF_NA_Q:OFF_NA_Q + 256], w[:, OFF_NA_K:OFF_NA_K + 256], w[:, OFF_NA_V:OFF_NA_V + 256],
            w[:, OFF_SWA_Q:OFF_SWA_Q + 256], dup_heads(w[:, OFF_SWA_K:OFF_SWA_V]),
            dup_heads(w[:, OFF_SWA_V:OFF_MLA_CKV]), w[:, OFF_MLA_CKV:OFF_MLA_KR],
            jnp.concatenate([kr] * N_HEADS, axis=1), w[:, OFF_MLA_CQ:OFF_POOL], w[:, OFF_POOL:OFF_GATE]]
    return jnp.concatenate(cols, axis=1).astype(BF16)


def _block_diag(blocks):
    r, c = blocks[0].shape
    n = len(blocks)
    rows = [jnp.concatenate([blocks[i] if j == i else jnp.zeros((r, c), blocks[0].dtype) for j in range(n)], axis=1)
            for i in range(n)]
    return jnp.concatenate(rows, axis=0)


def _rope_tables(n, dim):
    t = jnp.arange(n, dtype=jnp.int32)
    row = (t // GRID_W).astype(F32)
    col = (t % GRID_W).astype(F32)
    n_freq = dim // 4
    inv = jnp.power(ROPE_BASE, -jnp.arange(n_freq, dtype=F32) / n_freq)
    ang = jnp.concatenate([row[:, None] * inv, col[:, None] * inv], axis=-1)
    cos, sin = jnp.cos(ang), jnp.sin(ang)
    reps = LANES // dim
    cos_t = jnp.tile(jnp.concatenate([cos, cos], axis=1), (1, reps))
    sin_t = jnp.tile(jnp.concatenate([-sin, sin], axis=1), (1, reps))
    return cos_t, sin_t


def _na_bias_tables(rpb, rows):
    T = ATT_TILE
    tr = T // GRID_W
    nt = rows // tr
    H = rpb.shape[0]
    pad = GRID_W - NA_COLS
    padded = jnp.pad(rpb.astype(F32), ((0, 0), (0, 0), (pad, pad)))
    toeplitz = jnp.stack([padded[:, :, GRID_W - 1 - qc:2 * GRID_W - 1 - qc] for qc in range(GRID_W)], axis=2)
    qc, kc = np.arange(GRID_W)[:, None], np.arange(GRID_W)[None, :]
    c0 = np.clip(qc - NA_COLS // 2, 0, GRID_W - NA_COLS)
    col_ok = (kc >= c0) & (kc < c0 + NA_COLS)
    blocks = jnp.where(jnp.asarray(col_ok)[None, None], toeplitz, NEG_INF)
    masked = jnp.full((H, GRID_W, GRID_W), NEG_INF, F32)
    tabs = []
    for i in (0, 1, nt - 1):
        rows_out = []
        for ql in range(tr):
            qr = tr * i + ql
            r0 = min(max(qr - NA_ROWS // 2, 0), rows - NA_ROWS)
            row = []
            for kl in range(3 * tr):
                kr = tr * (i - 1) + kl
                row.append(blocks[:, kr - qr + NA_ROWS - 1] if r0 <= kr < r0 + NA_ROWS else masked)
            rows_out.append(jnp.concatenate(row, axis=2))
        tabs.append(jnp.concatenate(rows_out, axis=1))
    return jnp.stack(tabs, axis=0)


def kernel(x, c, ctx, c_ctx, ada_w, ada_b, norm1_g, norm2_g, w_in, pool_w, pool_scale, na_rpb, swa_sink,
           mla_q_norm, mla_kv_norm, mla_w_uq, mla_w_uk, mla_w_uv, w_branch, w_out, ffn_w1, ffn_w3, ffn_w2,
           final_norm_g):
    B, S, D = x.shape
    Lc = ctx.shape[1]
    depth = ada_w.shape[0]
    rows = S // GRID_W
    assert S % 1024 == 0 and rows >= 3 * (ATT_TILE // GRID_W) and Lc == ATT_TILE and B <= 7
    tm_x, tm_c = 512, Lc

    cond = jnp.concatenate([c, c_ctx[None], jnp.zeros((8 - B - 1, D), F32)], axis=0)
    mods = _modulation(cond, ada_w.astype(BF16), ada_b[:, None, :])
    rope_tabs = _rope_tables(S, HEAD_DIM) + _rope_tables(S, MLA_ROPE)
    row2 = lambda v: v.reshape(1, -1)

    xc = ctx
    for l in range(depth):
        last = l == depth - 1
        mod_x = mods[l, :B].reshape(B, 6, D)
        mod_c = mods[l, B:B + 1].reshape(1, 6, D)
        wp = _pack_w_in(w_in[l])
        wg = w_in[l][:, OFF_GATE:].astype(BF16)
        uq = mla_w_uq[l].reshape(-1, N_HEADS, MLA_NOPE + MLA_ROPE)
        wuq = jnp.concatenate([uq[:, :, :MLA_NOPE].reshape(-1, N_HEADS * MLA_NOPE),
                               uq[:, :, MLA_NOPE:].reshape(-1, N_HEADS * MLA_ROPE)], axis=1).astype(BF16)
        uk = jnp.transpose(mla_w_uk[l], (1, 2, 0))
        wuk = _block_diag([uk[h] for h in range(N_HEADS)]).astype(BF16)
        wuvt = jnp.transpose(mla_w_uv[l], (1, 2, 0)).astype(BF16)
        w_pool = _block_diag([pool_w[l, g] for g in range(len(POOL_WINDOWS))]).astype(BF16)
        sink_tab = jnp.concatenate([jnp.broadcast_to(swa_sink[l][:, None], (N_HEADS, ATT_TILE)),
                                    jnp.zeros((8 - N_HEADS, ATT_TILE), F32)], axis=0)
        bias = _na_bias_tables(na_rpb[l], rows)
        g1, g2 = row2(norm1_g[l]), row2(norm2_g[l])
        qg, kvg, psc = row2(mla_q_norm[l]), row2(mla_kv_norm[l]), row2(pool_scale[l])
        wb, wo = w_branch[l].astype(BF16), w_out[l].astype(BF16)
        w1, w3, w2 = ffn_w1[l].astype(BF16), ffn_w3[l].astype(BF16), ffn_w2[l].astype(BF16)

        (naq, nak, nav, swq, swk, _, kcat, ckvt, qmla, pool_in, swvt) = _inproj(
            x, mod_x, g1, wp, wuq, wuk, qg, kvg, rope_tabs, tm_x)
        (c_naq, c_nak, c_nav, c_swq, c_swk, c_swv, c_kcat, c_ckvt, c_qmla, c_pool_in, c_swvt) = _inproj(
            xc, mod_c, g1, wp, wuq, wuk, qg, kvg, None, tm_c)

        ys = [_pool(pool_in, w_pool, psc, 512),
              _na(naq, nak, nav, c_nak, c_nav, bias),
              _swa(swq, swk, swvt, c_swk, c_swvt, sink_tab),
              _mla(qmla, kcat, ckvt, c_kcat, c_ckvt, wuvt)]
        if not last:
            ys_c = [_pool(c_pool_in, w_pool, psc, Lc)] + list(_ctx_attn(
                c_naq, c_nak, c_nav, c_swq, c_swk, c_swv, c_qmla, c_kcat, c_ckvt, sink_tab, wuvt))
            xc = _merge(xc, mod_c, g1, ys_c, wg, wb, wo, tm_c)
            xc = _ffn(xc, mod_c, g2, w1, w3, w2, None, tm_c)
        x = _merge(x, mod_x, g1, ys, wg, wb, wo, tm_x)
        x = _ffn(x, mod_x, g2, w1, w3, w2, row2(final_norm_g) if last else None, tm_x)
    return x
```

```python
import functools

import numpy as np
import jax
import jax.numpy as jnp
from jax import lax
from jax.experimental import pallas as pl
from jax.experimental.pallas import tpu as pltpu

F32 = jnp.float32
BF16 = jnp.bfloat16

GRID_W = 64
HEAD_DIM = 64
ROPE_BASE = 10000.0
NEG_INF = -1e30
EPS = 1e-6
POOL_WINDOWS = (2, 4, 8, 16)
POOL_GROUP_DIM = 64
N_HEADS = 4
NA_ROWS = 8
NA_COLS = 16
SWA_WINDOW = 128
MLA_KV_RANK = 128
MLA_NOPE = 64
MLA_ROPE = 32
MLA_SCALE = (MLA_NOPE + MLA_ROPE) ** -0.5
LOG2_E = 1.4426950408889634
MLA_Q_SCALE = MLA_SCALE * LOG2_E
MLA_SUM_ROWS = 16
MLA_LAG_LIMIT = 64.0
ATT_SCALE = HEAD_DIM ** -0.5
BRANCH_W = 256
N_BRANCH = 4

OFF_NA_K, OFF_NA_V, OFF_SWA_K, OFF_SWA_V, OFF_MLA_CKV, OFF_MLA_KR = 0, 256, 512, 640, 768, 896
KV_COLS = 928
OFF_NA_Q, OFF_SWA_Q, OFF_MLA_CQ, OFF_POOL, OFF_GATE = 928, 1184, 1440, 1696, 1952

LANES = 128
ATT_TILE = 256
MLA_TQ = 256
MLA_TK = 512
MLA_UNROLL = 8
POOL_HALO = 8
VMEM_LIMIT = 56 * 1024 * 1024


def _params(sem):
    return pltpu.CompilerParams(dimension_semantics=sem, vmem_limit_bytes=VMEM_LIMIT)


def _dot(a, b):
    return jnp.dot(a, b, preferred_element_type=F32)


def _dot_nt(a, b):
    return lax.dot_general(a, b, (((1,), (1,)), ((), ())), preferred_element_type=F32)


def _resident(shape):
    n = len(shape)
    return pl.BlockSpec(shape, lambda *_: (0,) * n, pipeline_mode=pl.Buffered(1))


def _modulated_norm(x, g, shift, scale):
    y = x * lax.rsqrt(jnp.mean(x * x, axis=-1, keepdims=True) + EPS)
    return (y * g) * (1.0 + scale) + shift


def _rms(x, g):
    return x * lax.rsqrt(jnp.mean(x * x, axis=-1, keepdims=True) + EPS) * g


def _rope_lanes(t, cos, sin_signed, half):
    lane = lax.broadcasted_iota(jnp.int32, t.shape, 1)
    up = pltpu.roll(t, LANES - half, 1)
    down = pltpu.roll(t, half, 1)
    partner = jnp.where((lane & (2 * half - 1)) < half, up, down)
    return t * cos + partner * sin_signed


def _mod_kernel(c_ref, w_ref, b_ref, o_ref):
    c = c_ref[...]
    s = (c * jax.nn.sigmoid(c)).astype(BF16)
    o_ref[0] = _dot(s, w_ref[0].astype(BF16)) + b_ref[0]


def _modulation(cond, ada_w, ada_b):
    L, D, N = ada_w.shape
    tn = N // 4
    return pl.pallas_call(
        _mod_kernel,
        out_shape=jax.ShapeDtypeStruct((L, cond.shape[0], N), F32),
        grid=(L, N // tn),
        in_specs=[pl.BlockSpec(cond.shape, lambda l, j: (0, 0)),
                  pl.BlockSpec((1, D, tn), lambda l, j: (l, 0, j)),
                  pl.BlockSpec((1, 1, tn), lambda l, j: (l, 0, j))],
        out_specs=pl.BlockSpec((1, cond.shape[0], tn), lambda l, j: (l, 0, j)),
        compiler_params=_params(("arbitrary", "arbitrary")),
        name="modulation",
    )(cond, ada_w, ada_b)


def _inproj_kernel(rope, x_ref, mod_ref, g_ref, w_ref, wuq_ref, wuk_ref, qg_ref, kvg_ref, *rest):
    if rope:
        c64_ref, s64_ref, c32_ref, s32_ref = rest[:4]
        rest = rest[4:]
    (naq_ref, nak_ref, nav_ref, swq_ref, swk_ref, swv_ref, kcat_ref, ckvt_ref, qmla_ref, pool_ref,
     swvt_ref) = rest

    x = x_ref[0]
    h = _modulated_norm(x, g_ref[...], mod_ref[0, 0:1, :], mod_ref[0, 1:2, :]).astype(BF16)

    def seg(a, n):
        return _dot(h, w_ref[:, a:a + n])

    def rope64(t):
        if not rope:
            return t
        c, s = c64_ref[...], s64_ref[...]
        return jnp.concatenate([_rope_lanes(t[:, :LANES], c, s, 32), _rope_lanes(t[:, LANES:], c, s, 32)], axis=1)

    def rope32(t):
        return _rope_lanes(t, c32_ref[...], s32_ref[...], 16) if rope else t

    naq_ref[0] = (seg(0, 256) * ATT_SCALE).astype(BF16)
    nak_ref[0] = seg(256, 256).astype(BF16)
    nav_ref[0] = seg(512, 256).astype(BF16)
    swq_ref[0] = (rope64(seg(768, 256)) * (ATT_SCALE * LOG2_E)).astype(BF16)
    swk_ref[0] = rope64(seg(1024, 256)).astype(BF16)
    swv = seg(1280, 256)
    swv_ref[0] = swv.astype(BF16)
    swvt_ref[0] = swv.T.astype(BF16)

    ckv = _rms(seg(1536, 128), kvg_ref[...])
    kcat_ref[0, :, 0:LANES] = ckv.astype(BF16)
    kcat_ref[0, :, LANES:2 * LANES] = rope32(seg(1664, 128)).astype(BF16)
    ckvt_ref[0, 0:MLA_KV_RANK, :] = ckv.T.astype(BF16)
    ckvt_ref[0, MLA_KV_RANK:, :] = jnp.ones((MLA_SUM_ROWS, ckv.shape[0]), BF16)

    cq = _rms(seg(1792, 256), qg_ref[...]).astype(BF16)
    qq = _dot(cq, wuq_ref[...])
    q_lat = _dot(qq[:, :256].astype(BF16), wuk_ref[...])
    q_rope = rope32(qq[:, 256:384])
    lane = lax.broadcasted_iota(jnp.int32, q_rope.shape, 1)
    for hd in range(N_HEADS):
        qmla_ref[0, hd, :, 0:LANES] = (q_lat[:, hd * LANES:(hd + 1) * LANES] * MLA_Q_SCALE).astype(BF16)
        own = (lane >= hd * MLA_ROPE) & (lane < (hd + 1) * MLA_ROPE)
        qmla_ref[0, hd, :, LANES:2 * LANES] = (jnp.where(own, q_rope, 0.0) * MLA_Q_SCALE).astype(BF16)

    pool_ref[0] = seg(2048, 256)


def _inproj(x, mod, g, wp, wuq, wuk, qg, kvg, rope_tabs, tm):
    B, n, D = x.shape
    rope = rope_tabs is not None
    mod_b = mod.shape[0]
    tok = lambda w: pl.BlockSpec((1, tm, w), lambda b, i: (b, i, 0))
    in_specs = [tok(D),
                pl.BlockSpec((1, 6, D), (lambda b, i: (b, 0, 0)) if mod_b > 1 else (lambda b, i: (0, 0, 0))),
                _resident(g.shape), _resident(wp.shape), _resident(wuq.shape), _resident(wuk.shape),
                _resident(qg.shape), _resident(kvg.shape)]
    args = [x, mod, g, wp, wuq, wuk, qg, kvg]
    if rope:
        in_specs += [pl.BlockSpec((tm, LANES), lambda b, i: (i, 0))] * 4
        args += list(rope_tabs)
    bf = lambda w: jax.ShapeDtypeStruct((B, n, w), BF16)
    vt_rows = MLA_KV_RANK + MLA_SUM_ROWS
    out_shape = [bf(256)] * 7 + [jax.ShapeDtypeStruct((B, vt_rows, n), BF16),
                                 jax.ShapeDtypeStruct((B, N_HEADS, n, 256), BF16),
                                 jax.ShapeDtypeStruct((B, n, 256), F32), jax.ShapeDtypeStruct((B, 256, n), BF16)]
    out_specs = [tok(256)] * 7 + [pl.BlockSpec((1, vt_rows, tm), lambda b, i: (b, 0, i)),
                                  pl.BlockSpec((1, N_HEADS, tm, 256), lambda b, i: (b, 0, i, 0)),
                                  tok(256), pl.BlockSpec((1, 256, tm), lambda b, i: (b, 0, i))]
    return pl.pallas_call(
        functools.partial(_inproj_kernel, rope),
        out_shape=out_shape, grid=(B, n // tm), in_specs=in_specs, out_specs=out_specs,
        compiler_params=_params(("arbitrary", "arbitrary")),
        name="inproj_rope" if rope else "inproj_ctx",
    )(*args)


def _pool_kernel(n, tp, u_ref, prev_ref, next_ref, w_ref, sc_ref, o_ref, ext_ref):
    i = pl.program_id(1)
    last = pl.num_programs(1) - 1
    ext_ref[0:POOL_HALO, :] = jnp.where(i > 0, prev_ref[0], 0.0)
    ext_ref[POOL_HALO:POOL_HALO + tp, :] = u_ref[0]
    ext_ref[POOL_HALO + tp:, :] = jnp.where(i < last, next_ref[0], 0.0)

    t = i * tp + lax.broadcasted_iota(jnp.int32, (tp, LANES), 0)
    lane = lax.broadcasted_iota(jnp.int32, (tp, LANES), 1)

    def window_mean(col, w):
        acc = None
        for k in range(-(w // 2), w - w // 2):
            v = ext_ref[POOL_HALO + k:POOL_HALO + k + tp, col * LANES:(col + 1) * LANES]
            acc = v if acc is None else acc + v
        lo = jnp.clip(t - w // 2, 0, n)
        hi = jnp.clip(t - w // 2 + w, 0, n)
        return acc / (hi - lo).astype(F32)

    first = lane < POOL_GROUP_DIM
    m01 = jnp.where(first, window_mean(0, POOL_WINDOWS[0]), window_mean(0, POOL_WINDOWS[1]))
    m23 = jnp.where(first, window_mean(1, POOL_WINDOWS[2]), window_mean(1, POOL_WINDOWS[3]))
    y = (jnp.concatenate([m01, m23], axis=1) - u_ref[0]).astype(BF16)
    o_ref[0] = (_dot(y, w_ref[...]) * sc_ref[...]).astype(BF16)


def _pool(u, w_bd, scale, tp):
    B, n, W = u.shape
    hb = tp // POOL_HALO
    nh = n // POOL_HALO
    return pl.pallas_call(
        functools.partial(_pool_kernel, n, tp),
        out_shape=jax.ShapeDtypeStruct((B, n, W), BF16),
        grid=(B, n // tp),
        in_specs=[pl.BlockSpec((1, tp, W), lambda b, i: (b, i, 0)),
                  pl.BlockSpec((1, POOL_HALO, W), lambda b, i: (b, jnp.maximum(i * hb - 1, 0), 0)),
                  pl.BlockSpec((1, POOL_HALO, W), lambda b, i: (b, jnp.minimum((i + 1) * hb, nh - 1), 0)),
                  _resident(w_bd.shape), _resident(scale.shape)],
        out_specs=pl.BlockSpec((1, tp, W), lambda b, i: (b, i, 0)),
        scratch_shapes=[pltpu.VMEM((tp + 2 * POOL_HALO, W), F32)],
        compiler_params=_params(("arbitrary", "arbitrary")),
        name="pool",
    )(u, u, u, w_bd, scale)


def _head_mask(shape, hd):
    lane = lax.broadcasted_iota(jnp.int32, shape, 1)
    return (lane >= hd * HEAD_DIM) & (lane < (hd + 1) * HEAD_DIM)


def _exp_fn(log2_units):
    return jnp.exp2 if log2_units else jnp.exp


def _attend_heads(q, key_parts, val_parts, fix_parts, sink_ref, log2_units=False):
    exp = _exp_fn(log2_units)
    out = jnp.zeros(q.shape, F32)
    for hd in range(N_HEADS):
        qh = jnp.where(_head_mask(q.shape, hd), q, jnp.zeros_like(q))
        scores = []
        for kp, fix in zip(key_parts, fix_parts):
            s = _dot_nt(qh, kp)
            scores.append(s if fix is None else fix(hd, s))
        m = functools.reduce(jnp.maximum, [jnp.max(s, axis=-1, keepdims=True) for s in scores])
        if sink_ref is not None:
            sink = sink_ref[hd:hd + 1, 0:1] * (LOG2_E if log2_units else 1.0)
            m = jnp.maximum(m, sink)
        probs = [exp(s - m) for s in scores]
        den = functools.reduce(jnp.add, [jnp.sum(p, axis=-1, keepdims=True) for p in probs])
        if sink_ref is not None:
            den = den + exp(sink - m)
        o = functools.reduce(jnp.add, [_dot(p.astype(BF16), vp) for p, vp in zip(probs, val_parts)])
        out = jnp.where(_head_mask(out.shape, hd), o / den, out)
    return out


def _attend_heads_t(q, key_parts, valt_parts, fix_parts, sink_ref, log2_units=False):
    exp = _exp_fn(log2_units)
    outs = []
    for hd in range(N_HEADS):
        qh = jnp.where(_head_mask(q.shape, hd), q, jnp.zeros_like(q))
        scores = []
        for kp, fix in zip(key_parts, fix_parts):
            s = _dot_nt(kp, qh)
            scores.append(s if fix is None else fix(hd, s))
        m = functools.reduce(jnp.maximum, [jnp.max(s, axis=0, keepdims=True) for s in scores])
        if sink_ref is not None:
            sink = sink_ref[hd:hd + 1, :] * (LOG2_E if log2_units else 1.0)
            m = jnp.maximum(m, sink)
        probs = [exp(s - m) for s in scores]
        den = functools.reduce(jnp.add, [jnp.sum(p, axis=0, keepdims=True) for p in probs])
        if sink_ref is not None:
            den = den + exp(sink - m)
        o = functools.reduce(jnp.add, [_dot(vt[hd * HEAD_DIM:(hd + 1) * HEAD_DIM, :], p.astype(BF16))
                                       for p, vt in zip(probs, valt_parts)])
        outs.append(o / den)
    return jnp.concatenate(outs, axis=0).T


def _na_kernel(q_ref, k0_ref, k1_ref, k2_ref, v0_ref, v1_ref, v2_ref, kc_ref, vc_ref, bias_ref, o_ref):
    T = ATT_TILE
    fixes = [(lambda hd, s, j=j: s + bias_ref[0, hd, :, j * T:(j + 1) * T]) for j in range(3)] + [None]
    out = _attend_heads(q_ref[0], [k0_ref[0], k1_ref[0], k2_ref[0], kc_ref[0]],
                        [v0_ref[0], v1_ref[0], v2_ref[0], vc_ref[0]], fixes, None)
    o_ref[0] = out.astype(BF16)


def _neighbour_specs(nt):
    T = ATT_TILE
    lo, mid, hi = (lambda i: jnp.maximum(i - 1, 0)), (lambda i: i), (lambda i: jnp.minimum(i + 1, nt - 1))
    return [pl.BlockSpec((1, T, 256), lambda b, i, f=f: (b, f(i), 0)) for f in (lo, mid, hi)]


def _na(q, k, v, kc, vc, bias):
    B, S, W = q.shape
    T = ATT_TILE
    nt = S // T
    tile = pl.BlockSpec((1, T, W), lambda b, i: (b, i, 0))
    ctx = lambda a: pl.BlockSpec((1,) + a.shape[1:], lambda b, i: (b, 0, 0))
    variant = lambda b, i: (jnp.where(i == 0, 0, jnp.where(i == nt - 1, 2, 1)), 0, 0, 0)
    return pl.pallas_call(
        _na_kernel,
        out_shape=jax.ShapeDtypeStruct((B, S, W), BF16),
        grid=(B, nt),
        in_specs=[tile] + _neighbour_specs(nt) + _neighbour_specs(nt) + [ctx(kc), ctx(vc),
                  pl.BlockSpec((1,) + bias.shape[1:], variant)],
        out_specs=tile,
        compiler_params=_params(("arbitrary", "arbitrary")),
        name="neighbourhood_attention",
    )(q, k, k, k, v, v, v, kc, vc, bias)


def _swa_kernel(S, q_ref, k0_ref, k1_ref, k2_ref, v0_ref, v1_ref, v2_ref, kc_ref, vc_ref, sink_ref, o_ref):
    T = ATT_TILE
    i = pl.program_id(1)

    def band(first_key, n_keys):
        kpos = first_key + lax.broadcasted_iota(jnp.int32, (n_keys, T), 0)
        qpos = i * T + lax.broadcasted_iota(jnp.int32, (n_keys, T), 1)
        ok = (jnp.abs(kpos - qpos) <= SWA_WINDOW) & (kpos >= 0) & (kpos < S)
        return lambda hd, s: jnp.where(ok, s, NEG_INF)

    fixes = [band(i * T - SWA_WINDOW, SWA_WINDOW), band(i * T, T), band((i + 1) * T, SWA_WINDOW), None]
    out = _attend_heads_t(q_ref[0], [k0_ref[0], k1_ref[0], k2_ref[0], kc_ref[0]],
                          [v0_ref[0], v1_ref[0], v2_ref[0], vc_ref[0]], fixes, sink_ref, log2_units=True)
    o_ref[0] = out.astype(BF16)


def _swa(q, k, vt, kc, vtc, sink_tab):
    B, S, W = q.shape
    T = ATT_TILE
    nt = S // T
    tile = pl.BlockSpec((1, T, W), lambda b, i: (b, i, 0))
    ctx = lambda a: pl.BlockSpec((1,) + a.shape[1:], lambda b, i: (b, 0, 0))
    half = T // SWA_WINDOW
    prev = lambda i: jnp.maximum(half * i - 1, 0)
    nxt = lambda i: jnp.minimum(half * (i + 1), half * nt - 1)
    keys = [pl.BlockSpec((1, SWA_WINDOW, W), lambda b, i: (b, prev(i), 0)), tile,
            pl.BlockSpec((1, SWA_WINDOW, W), lambda b, i: (b, nxt(i), 0))]
    vals = [pl.BlockSpec((1, W, SWA_WINDOW), lambda b, i: (b, 0, prev(i))),
            pl.BlockSpec((1, W, T), lambda b, i: (b, 0, i)),
            pl.BlockSpec((1, W, SWA_WINDOW), lambda b, i: (b, 0, nxt(i)))]
    return pl.pallas_call(
        functools.partial(_swa_kernel, S),
        out_shape=jax.ShapeDtypeStruct((B, S, W), BF16),
        grid=(B, nt),
        in_specs=[tile] + keys + vals + [ctx(kc), ctx(vtc), _resident(sink_tab.shape)],
        out_specs=tile,
        compiler_params=_params(("arbitrary", "arbitrary")),
        name="windowed_attention",
    )(q, k, k, k, vt, vt, vt, kc, vtc, sink_tab)


def _mla_finish(acc, wuvt_ref, tq):
    o = (acc[0:MLA_KV_RANK] / acc[MLA_KV_RANK:MLA_KV_RANK + 1]).astype(BF16)
    outs = [_dot(wuvt_ref[hd], o[:, hd * tq:(hd + 1) * tq]) for hd in range(N_HEADS)]
    return jnp.concatenate(outs, axis=0).T


def _mla_kernel(n_chunks, unroll, tk, q_ref, k_ref, vt_ref, kc_ref, vtc_ref, wuvt_ref, o_ref,
                m_ref, ex_ref, acc_ref):
    tq = q_ref.shape[2]
    q = q_ref[0].reshape(N_HEADS * tq, q_ref.shape[3])

    def scores(c):
        off = pl.multiple_of(c * tk, tk)
        return _dot_nt(k_ref[0, pl.ds(off, tk), :], q)

    def values(c):
        return vt_ref[0, :, pl.ds(pl.multiple_of(c * tk, tk), tk)]

    def start():
        s = _dot_nt(kc_ref[0], q)
        m = jnp.max(s, axis=0, keepdims=True)
        m_ref[...] = m
        acc_ref[...] = _dot(vtc_ref[0], jnp.exp2(s - m).astype(BF16))

    def lagged(s, vt):
        m_old = m_ref[...]
        p = jnp.exp2(s - m_old).astype(BF16)
        cm = jnp.max(s, axis=0, keepdims=True)
        m_new = jnp.maximum(m_old, cm)
        ex_ref[...] = jnp.maximum(ex_ref[...], cm - m_old)
        m_ref[...] = m_new
        acc_ref[...] = (acc_ref[...] + _dot(vt, p)) * jnp.exp2(m_old - m_new)

    def exact(c, carry):
        s = scores(c)
        m_old = m_ref[...]
        m_new = jnp.maximum(m_old, jnp.max(s, axis=0, keepdims=True))
        p = jnp.exp2(s - m_new).astype(BF16)
        m_ref[...] = m_new
        acc_ref[...] = jnp.exp2(m_old - m_new) * acc_ref[...] + _dot(values(c), p)
        return carry

    def group(cc, carry):
        m_old = m_ref[...]
        pv, cm = None, None
        for j in range(unroll):
            c = unroll * cc + j
            s = scores(c)
            p = jnp.exp2(s - m_old).astype(BF16)
            cj = jnp.max(s, axis=0, keepdims=True)
            cm = cj if cm is None else jnp.maximum(cm, cj)
            d = _dot(values(c), p)
            pv = d if pv is None else pv + d
        m_new = jnp.maximum(m_old, cm)
        ex_ref[...] = jnp.maximum(ex_ref[...], cm - m_old)
        m_ref[...] = m_new
        acc_ref[...] = (acc_ref[...] + pv) * jnp.exp2(m_old - m_new)
        return carry

    start()
    ex_ref[...] = jnp.zeros(ex_ref.shape, F32)
    lax.fori_loop(0, n_chunks // unroll, group, 0)

    @pl.when(jnp.max(ex_ref[...]) > MLA_LAG_LIMIT)
    def _():
        start()
        lax.fori_loop(0, n_chunks, exact, 0)

    o_ref[0] = _mla_finish(acc_ref[...], wuvt_ref, tq).astype(BF16)


def _mla(q, k, vt, kc, vtc, wuvt):
    B, H, S, W = q.shape
    tq, tk = MLA_TQ, MLA_TK
    n_chunks = S // tk
    assert S % tk == 0
    unroll = max(u for u in range(1, MLA_UNROLL + 1) if n_chunks % u == 0)
    per_batch = lambda shape: pl.BlockSpec((1,) + shape[1:], lambda b, i: (b,) + (0,) * (len(shape) - 1))
    return pl.pallas_call(
        functools.partial(_mla_kernel, n_chunks, unroll, tk),
        out_shape=jax.ShapeDtypeStruct((B, S, BRANCH_W), BF16),
        grid=(B, S // tq),
        in_specs=[pl.BlockSpec((1, H, tq, W), lambda b, i: (b, 0, i, 0)),
                  per_batch(k.shape), per_batch(vt.shape), per_batch(kc.shape), per_batch(vtc.shape),
                  _resident(wuvt.shape)],
        out_specs=pl.BlockSpec((1, tq, BRANCH_W), lambda b, i: (b, i, 0)),
        scratch_shapes=[pltpu.VMEM((1, H * tq), F32), pltpu.VMEM((1, H * tq), F32),
                        pltpu.VMEM((vt.shape[1], H * tq), F32)],
        compiler_params=_params(("arbitrary", "arbitrary")),
        name="latent_attention",
    )(q, k, vt, kc, vtc, wuvt)


def _ctx_attn_kernel(naq_ref, nak_ref, nav_ref, swq_ref, swk_ref, swv_ref, qmla_ref, kcat_ref, ckvt_ref,
                     sink_ref, wuvt_ref, na_o_ref, sw_o_ref, mla_o_ref):
    na_o_ref[0] = _attend_heads(naq_ref[0], [nak_ref[0]], [nav_ref[0]], [None], None).astype(BF16)
    sw_o_ref[0] = _attend_heads(swq_ref[0], [swk_ref[0]], [swv_ref[0]], [None], sink_ref,
                                log2_units=True).astype(BF16)
    tq = qmla_ref.shape[2]
    q = qmla_ref[0].reshape(N_HEADS * tq, qmla_ref.shape[3])
    s = _dot_nt(kcat_ref[0], q)
    p = jnp.exp2(s - jnp.max(s, axis=0, keepdims=True))
    acc = _dot(ckvt_ref[0], p.astype(BF16))
    mla_o_ref[0] = _mla_finish(acc, wuvt_ref, tq).astype(BF16)


def _ctx_attn(naq, nak, nav, swq, swk, swv, qmla, kcat, ckvt, sink_tab, wuvt):
    B, Lc, W = naq.shape
    per_batch = lambda a: pl.BlockSpec((1,) + a.shape[1:], lambda b: (b,) + (0,) * (a.ndim - 1))
    acts = [naq, nak, nav, swq, swk, swv, qmla, kcat, ckvt]
    out = jax.ShapeDtypeStruct((B, Lc, W), BF16)
    return pl.pallas_call(
        _ctx_attn_kernel,
        out_shape=[out, out, out],
        grid=(B,),
        in_specs=[per_batch(a) for a in acts] + [_resident(sink_tab.shape), _resident(wuvt.shape)],
        out_specs=[per_batch(naq)] * 3,
        compiler_params=_params(("arbitrary",)),
        name="context_attention",
    )(*acts, sink_tab, wuvt)


def _merge_kernel(x_ref, mod_ref, g_ref, y0_ref, y1_ref, y2_ref, y3_ref, wg_ref, wb_ref, wo_ref, o_ref):
    x = x_ref[0]
    D = x.shape[1]
    h = _modulated_norm(x, g_ref[...], mod_ref[0, 0:1, :], mod_ref[0, 1:2, :]).astype(BF16)
    merged = None
    for i, y_ref in enumerate((y0_ref, y1_ref, y2_ref, y3_ref)):
        gate = jax.nn.sigmoid(_dot(h, wg_ref[:, i * D:(i + 1) * D]))
        term = gate * _dot(y_ref[0], wb_ref[i])
        merged = term if merged is None else merged + term
    o_ref[0] = x + mod_ref[0, 2:3, :] * _dot(merged.astype(BF16), wo_ref[...])


def _mod_spec(mod, D):
    return pl.BlockSpec((1, 6, D), (lambda b, i: (b, 0, 0)) if mod.shape[0] > 1 else (lambda b, i: (0, 0, 0)))


def _merge(x, mod, g, ys, wg, wb, wo, tm):
    B, n, D = x.shape
    tok = lambda w: pl.BlockSpec((1, tm, w), lambda b, i: (b, i, 0))
    return pl.pallas_call(
        _merge_kernel,
        out_shape=jax.ShapeDtypeStruct((B, n, D), F32),
        grid=(B, n // tm),
        in_specs=[tok(D), _mod_spec(mod, D), _resident(g.shape)] + [tok(BRANCH_W)] * 4
                 + [_resident(wg.shape), _resident(wb.shape), _resident(wo.shape)],
        out_specs=tok(D),
        compiler_params=_params(("arbitrary", "arbitrary")),
        name="merge",
    )(x, mod, g, *ys, wg, wb, wo)


def _ffn_kernel(final, x_ref, mod_ref, g_ref, w1_ref, w3_ref, w2_ref, *rest):
    if final:
        fg_ref, o_ref = rest
    else:
        (o_ref,) = rest
    x = x_ref[0]
    h = _modulated_norm(x, g_ref[...], mod_ref[0, 3:4, :], mod_ref[0, 4:5, :]).astype(BF16)
    a = _dot(h, w1_ref[...])
    act = ((a * jax.nn.sigmoid(a)) * _dot(h, w3_ref[...])).astype(BF16)
    y = x + mod_ref[0, 5:6, :] * _dot(act, w2_ref[...])
    o_ref[0] = _rms(y, fg_ref[...]) if final else y


def _ffn(x, mod, g, w1, w3, w2, final_g, tm):
    B, n, D = x.shape
    final = final_g is not None
    tok = pl.BlockSpec((1, tm, D), lambda b, i: (b, i, 0))
    in_specs = [tok, _mod_spec(mod, D), _resident(g.shape), _resident(w1.shape), _resident(w3.shape),
                _resident(w2.shape)]
    args = [x, mod, g, w1, w3, w2]
    if final:
        in_specs.append(_resident(final_g.shape))
        args.append(final_g)
    return pl.pallas_call(
        functools.partial(_ffn_kernel, final),
        out_shape=jax.ShapeDtypeStruct((B, n, D), F32),
        grid=(B, n // tm),
        in_specs=in_specs, out_specs=tok,
        compiler_params=_params(("arbitrary", "arbitrary")),
        name="ffn_final" if final else "ffn",
    )(*args)


def _pack_w_in(w):
    def dup_heads(a):
        return jnp.concatenate([a[:, :64], a[:, :64], a[:, 64:], a[:, 64:]], axis=1)

    w = w.astype(BF16)
    kr = w[:, OFF_MLA_KR:KV_COLS]
    cols = [w[:, OFF_NA_Q:OFF_NA_Q + 256], w[:, OFF_NA_K:OFF_NA_K + 256], w[:, OFF_NA_V:OFF_NA_V + 256],
            w[:, OFF_SWA_Q:OFF_SWA_Q + 256], dup_heads(w[:, OFF_SWA_K:OFF_SWA_V]),
            dup_heads(w[:, OFF_SWA_V:OFF_MLA_CKV]), w[:, OFF_MLA_CKV:OFF_MLA_KR],
            jnp.concatenate([kr] * N_HEADS, axis=1), w[:, OFF_MLA_CQ:OFF_POOL], w[:, OFF_POOL:OFF_GATE]]
    return jnp.concatenate(cols, axis=1), w[:, OFF_GATE:]


def _block_diag(blocks):
    r, c = blocks[0].shape
    n = len(blocks)
    rows = [jnp.concatenate([blocks[i] if j == i else jnp.zeros((r, c), blocks[0].dtype) for j in range(n)], axis=1)
            for i in range(n)]
    return jnp.concatenate(rows, axis=0)


def _rope_tables(n, dim):
    rows = n // GRID_W
    n_freq = dim // 4
    inv = jnp.power(ROPE_BASE, -jnp.arange(n_freq, dtype=F32) / n_freq)
    row_ang = jnp.arange(rows, dtype=jnp.int32).astype(F32)[:, None] * inv
    col_ang = jnp.arange(GRID_W, dtype=jnp.int32).astype(F32)[:, None] * inv

    def per_token(f):
        r = jnp.broadcast_to(f(row_ang)[:, None, :], (rows, GRID_W, n_freq))
        c = jnp.broadcast_to(f(col_ang)[None, :, :], (rows, GRID_W, n_freq))
        return jnp.concatenate([r, c], axis=-1).reshape(n, 2 * n_freq)

    cos, sin = per_token(jnp.cos), per_token(jnp.sin)
    reps = LANES // dim
    cos_t = jnp.tile(jnp.concatenate([cos, cos], axis=1), (1, reps))
    sin_t = jnp.tile(jnp.concatenate([-sin, sin], axis=1), (1, reps))
    return cos_t, sin_t


def _na_bias_tables(rpb, rows):
    T = ATT_TILE
    tr = T // GRID_W
    nt = rows // tr
    H = rpb.shape[0]
    pad = GRID_W - NA_COLS
    padded = jnp.pad(rpb.astype(F32), ((0, 0), (0, 0), (pad, pad)))
    toeplitz = jnp.stack([padded[:, :, GRID_W - 1 - qc:2 * GRID_W - 1 - qc] for qc in range(GRID_W)], axis=2)
    qc, kc = np.arange(GRID_W)[:, None], np.arange(GRID_W)[None, :]
    c0 = np.clip(qc - NA_COLS // 2, 0, GRID_W - NA_COLS)
    col_ok = (kc >= c0) & (kc < c0 + NA_COLS)
    blocks = jnp.where(jnp.asarray(col_ok)[None, None], toeplitz, NEG_INF)
    n_off = 2 * NA_ROWS - 1
    blocks = jnp.concatenate([blocks, jnp.full((H, 1, GRID_W, GRID_W), NEG_INF, F32)], axis=1)
    which = np.full((3, tr, 3 * tr), n_off, np.int32)
    for v, i in enumerate((0, 1, nt - 1)):
        for ql in range(tr):
            qr = tr * i + ql
            r0 = min(max(qr - NA_ROWS // 2, 0), rows - NA_ROWS)
            for kl in range(3 * tr):
                kr = tr * (i - 1) + kl
                if r0 <= kr < r0 + NA_ROWS:
                    which[v, ql, kl] = kr - qr + NA_ROWS - 1
    tab = jnp.take(blocks, jnp.asarray(which.reshape(-1)), axis=1)
    tab = tab.reshape(H, 3, tr, 3 * tr, GRID_W, GRID_W)
    return jnp.transpose(tab, (1, 0, 2, 4, 3, 5)).reshape(3, H, T, 3 * T)


def kernel(x, c, ctx, c_ctx, ada_w, ada_b, norm1_g, norm2_g, w_in, pool_w, pool_scale, na_rpb, swa_sink,
           mla_q_norm, mla_kv_norm, mla_w_uq, mla_w_uk, mla_w_uv, w_branch, w_out, ffn_w1, ffn_w3, ffn_w2,
           final_norm_g):
    B, S, D = x.shape
    Lc = ctx.shape[1]
    depth = ada_w.shape[0]
    rows = S // GRID_W
    assert S % 1024 == 0 and rows >= 3 * (ATT_TILE // GRID_W) and Lc == ATT_TILE and B <= 7
    tm_x, tm_c = 512, Lc

    cond = jnp.concatenate([c, c_ctx[None], jnp.zeros((8 - B - 1, D), F32)], axis=0)
    mods = _modulation(cond, ada_w, ada_b[:, None, :])
    rope_tabs = _rope_tables(S, HEAD_DIM) + _rope_tables(S, MLA_ROPE)
    row2 = lambda v: v.reshape(1, -1)

    xc = ctx
    for l in range(depth):
        last = l == depth - 1
        mod_x = mods[l, :B].reshape(B, 6, D)
        mod_c = mods[l, B:B + 1].reshape(1, 6, D)
        wp, wg = _pack_w_in(w_in[l])
        uq = mla_w_uq[l].reshape(-1, N_HEADS, MLA_NOPE + MLA_ROPE)
        wuq = jnp.concatenate([uq[:, :, :MLA_NOPE].reshape(-1, N_HEADS * MLA_NOPE),
                               uq[:, :, MLA_NOPE:].reshape(-1, N_HEADS * MLA_ROPE)], axis=1).astype(BF16)
        uk = jnp.transpose(mla_w_uk[l], (1, 2, 0))
        wuk = _block_diag([uk[h] for h in range(N_HEADS)]).astype(BF16)
        wuvt = jnp.transpose(mla_w_uv[l], (1, 2, 0)).astype(BF16)
        w_pool = _block_diag([pool_w[l, g] for g in range(len(POOL_WINDOWS))]).astype(BF16)
        sink_tab = jnp.concatenate([jnp.broadcast_to(swa_sink[l][:, None], (N_HEADS, ATT_TILE)),
                                    jnp.zeros((8 - N_HEADS, ATT_TILE), F32)], axis=0)
        bias = _na_bias_tables(na_rpb[l], rows)
        g1, g2 = row2(norm1_g[l]), row2(norm2_g[l])
        qg, kvg, psc = row2(mla_q_norm[l]), row2(mla_kv_norm[l]), row2(pool_scale[l])
        wb, wo = w_branch[l].astype(BF16), w_out[l].astype(BF16)
        w1, w3, w2 = ffn_w1[l].astype(BF16), ffn_w3[l].astype(BF16), ffn_w2[l].astype(BF16)

        (naq, nak, nav, swq, swk, _, kcat, ckvt, qmla, pool_in, swvt) = _inproj(
            x, mod_x, g1, wp, wuq, wuk, qg, kvg, rope_tabs, tm_x)
        (c_naq, c_nak, c_nav, c_swq, c_swk, c_swv, c_kcat, c_ckvt, c_qmla, c_pool_in, c_swvt) = _inproj(
            xc, mod_c, g1, wp, wuq, wuk, qg, kvg, None, tm_c)

        ys = [_pool(pool_in, w_pool, psc, 512),
              _na(naq, nak, nav, c_nak, c_nav, bias),
              _swa(swq, swk, swvt, c_swk, c_swvt, sink_tab),
              _mla(qmla, kcat, ckvt, c_kcat, c_ckvt, wuvt)]
        if not last:
            ys_c = [_pool(c_pool_in, w_pool, psc, Lc)] + list(_ctx_attn(
                c_naq, c_nak, c_nav, c_swq, c_swk, c_swv, c_qmla, c_kcat, c_ckvt, sink_tab, wuvt))
            xc = _merge(xc, mod_c, g1, ys_c, wg, wb, wo, tm_c)
            xc = _ffn(xc, mod_c, g2, w1, w3, w2, None, tm_c)
        x = _merge(x, mod_x, g1, ys, wg, wb, wo, tm_x)
        x = _ffn(x, mod_x, g2, w1, w3, w2, row2(final_norm_g) if last else None, tm_x)
    return x
```

```python
import functools

import numpy as np
import jax
import jax.numpy as jnp
from jax import lax
from jax.experimental import pallas as pl
from jax.experimental.pallas import tpu as pltpu

F32 = jnp.float32
BF16 = jnp.bfloat16

GRID_W = 64
HEAD_DIM = 64
ROPE_BASE = 10000.0
NEG_INF = -1e30
EPS = 1e-6
POOL_WINDOWS = (2, 4, 8, 16)
POOL_GROUP_DIM = 64
N_HEADS = 4
NA_ROWS = 8
NA_COLS = 16
SWA_WINDOW = 128
MLA_KV_RANK = 128
MLA_NOPE = 64
MLA_ROPE = 32
MLA_SCALE = (MLA_NOPE + MLA_ROPE) ** -0.5
LOG2_E = 1.4426950408889634
MLA_Q_SCALE = MLA_SCALE * LOG2_E
MLA_SUM_ROWS = 16
MLA_LAG_LIMIT = 64.0
ATT_SCALE = HEAD_DIM ** -0.5
BRANCH_W = 256
N_BRANCH = 4

OFF_NA_K, OFF_NA_V, OFF_SWA_K, OFF_SWA_V, OFF_MLA_CKV, OFF_MLA_KR = 0, 256, 512, 640, 768, 896
KV_COLS = 928
OFF_NA_Q, OFF_SWA_Q, OFF_MLA_CQ, OFF_POOL, OFF_GATE = 928, 1184, 1440, 1696, 1952

LANES = 128
ATT_TILE = 256
MLA_TQ = 256
MLA_TK = 512
MLA_UNROLL = 16
POOL_HALO = 8
VMEM_LIMIT = 56 * 1024 * 1024


def _params(sem):
    return pltpu.CompilerParams(dimension_semantics=sem, vmem_limit_bytes=VMEM_LIMIT)


def _dot(a, b):
    return jnp.dot(a, b, preferred_element_type=F32)


def _dot_nt(a, b):
    return lax.dot_general(a, b, (((1,), (1,)), ((), ())), preferred_element_type=F32)


def _resident(shape):
    n = len(shape)
    return pl.BlockSpec(shape, lambda *_: (0,) * n, pipeline_mode=pl.Buffered(1))


def _modulated_norm(x, g, shift, scale):
    y = x * lax.rsqrt(jnp.mean(x * x, axis=-1, keepdims=True) + EPS)
    return (y * g) * (1.0 + scale) + shift


def _rms(x, g):
    return x * lax.rsqrt(jnp.mean(x * x, axis=-1, keepdims=True) + EPS) * g


def _rope_lanes(t, cos, sin_signed, half):
    lane = lax.broadcasted_iota(jnp.int32, t.shape, 1)
    up = pltpu.roll(t, LANES - half, 1)
    down = pltpu.roll(t, half, 1)
    partner = jnp.where((lane & (2 * half - 1)) < half, up, down)
    return t * cos + partner * sin_signed


def _mod_kernel(c_ref, w_ref, b_ref, o_ref):
    c = c_ref[...]
    s = (c * jax.nn.sigmoid(c)).astype(BF16)
    o_ref[0] = _dot(s, w_ref[0].astype(BF16)) + b_ref[0]


def _modulation(cond, ada_w, ada_b):
    L, D, N = ada_w.shape
    tn = N // 4
    return pl.pallas_call(
        _mod_kernel,
        out_shape=jax.ShapeDtypeStruct((L, cond.shape[0], N), F32),
        grid=(L, N // tn),
        in_specs=[pl.BlockSpec(cond.shape, lambda l, j: (0, 0)),
                  pl.BlockSpec((1, D, tn), lambda l, j: (l, 0, j)),
                  pl.BlockSpec((1, 1, tn), lambda l, j: (l, 0, j))],
        out_specs=pl.BlockSpec((1, cond.shape[0], tn), lambda l, j: (l, 0, j)),
        compiler_params=_params(("arbitrary", "arbitrary")),
        name="modulation",
    )(cond, ada_w, ada_b)


def _inproj_kernel(rope, x_ref, mod_ref, g_ref, w_ref, wuq_ref, wuk_ref, qg_ref, kvg_ref, *rest):
    if rope:
        tm = x_ref.shape[1]
        g = tm // GRID_W

        def per_token(row_ref, col_ref):
            r = jnp.broadcast_to(row_ref[...][:, None, :], (g, GRID_W, LANES)).reshape(tm, LANES)
            c = jnp.broadcast_to(col_ref[...][None, :, :], (g, GRID_W, LANES)).reshape(tm, LANES)
            return r + c

        c64, s64 = per_token(rest[0], rest[2]), per_token(rest[1], rest[3])
        c32, s32 = per_token(rest[4], rest[6]), per_token(rest[5], rest[7])
        rest = rest[8:]
    (naq_ref, nak_ref, nav_ref, swq_ref, swk_ref, swv_ref, kcat_ref, ckvt_ref, qmla_ref, pool_ref,
     swvt_ref) = rest

    x = x_ref[0]
    h = _modulated_norm(x, g_ref[...], mod_ref[0, 0:1, :], mod_ref[0, 1:2, :]).astype(BF16)

    def seg(a, n):
        return _dot(h, w_ref[:, a:a + n])

    def rope64(t):
        if not rope:
            return t
        return jnp.concatenate([_rope_lanes(t[:, :LANES], c64, s64, 32),
                                _rope_lanes(t[:, LANES:], c64, s64, 32)], axis=1)

    def rope32(t):
        return _rope_lanes(t, c32, s32, 16) if rope else t

    naq_ref[0] = (seg(0, 256) * ATT_SCALE).astype(BF16)
    nak_ref[0] = seg(256, 256).astype(BF16)
    nav_ref[0] = seg(512, 256).astype(BF16)
    swq_ref[0] = (rope64(seg(768, 256)) * (ATT_SCALE * LOG2_E)).astype(BF16)
    swk_ref[0] = rope64(seg(1024, 256)).astype(BF16)
    swv = seg(1280, 256)
    swv_ref[0] = swv.astype(BF16)
    swvt_ref[0] = swv.T.astype(BF16)

    ckv = _rms(seg(1536, 128), kvg_ref[...])
    kcat_ref[0, :, 0:LANES] = ckv.astype(BF16)
    kcat_ref[0, :, LANES:2 * LANES] = rope32(seg(1664, 128)).astype(BF16)
    ckvt_ref[0, 0:MLA_KV_RANK, :] = ckv.T.astype(BF16)
    ckvt_ref[0, MLA_KV_RANK:, :] = jnp.ones((MLA_SUM_ROWS, ckv.shape[0]), BF16)

    cq = _rms(seg(1792, 256), qg_ref[...]).astype(BF16)
    qq = _dot(cq, wuq_ref[...])
    q_lat = _dot(qq[:, :256].astype(BF16), wuk_ref[...])
    q_rope = rope32(qq[:, 256:384])
    lane = lax.broadcasted_iota(jnp.int32, q_rope.shape, 1)
    for hd in range(N_HEADS):
        qmla_ref[0, hd, :, 0:LANES] = (q_lat[:, hd * LANES:(hd + 1) * LANES] * MLA_Q_SCALE).astype(BF16)
        own = (lane >= hd * MLA_ROPE) & (lane < (hd + 1) * MLA_ROPE)
        qmla_ref[0, hd, :, LANES:2 * LANES] = (jnp.where(own, q_rope, 0.0) * MLA_Q_SCALE).astype(BF16)

    pool_ref[0] = seg(2048, 256)


def _inproj(x, mod, g, wp, wuq, wuk, qg, kvg, rope_tabs, tm):
    B, n, D = x.shape
    rope = rope_tabs is not None
    mod_b = mod.shape[0]
    tok = lambda w: pl.BlockSpec((1, tm, w), lambda b, i: (b, i, 0))
    in_specs = [tok(D),
                pl.BlockSpec((1, 6, D), (lambda b, i: (b, 0, 0)) if mod_b > 1 else (lambda b, i: (0, 0, 0))),
                _resident(g.shape), _resident(wp.shape), _resident(wuq.shape), _resident(wuk.shape),
                _resident(qg.shape), _resident(kvg.shape)]
    args = [x, mod, g, wp, wuq, wuk, qg, kvg]
    if rope:
        assert tm % (8 * GRID_W) == 0
        row_tab = pl.BlockSpec((tm // GRID_W, LANES), lambda b, i: (i, 0))
        col_tab = _resident((GRID_W, LANES))
        in_specs += [row_tab, row_tab, col_tab, col_tab] * 2
        args += list(rope_tabs)
    bf = lambda w: jax.ShapeDtypeStruct((B, n, w), BF16)
    vt_rows = MLA_KV_RANK + MLA_SUM_ROWS
    out_shape = [bf(256)] * 7 + [jax.ShapeDtypeStruct((B, vt_rows, n), BF16),
                                 jax.ShapeDtypeStruct((B, N_HEADS, n, 256), BF16),
                                 jax.ShapeDtypeStruct((B, n, 256), F32), jax.ShapeDtypeStruct((B, 256, n), BF16)]
    out_specs = [tok(256)] * 7 + [pl.BlockSpec((1, vt_rows, tm), lambda b, i: (b, 0, i)),
                                  pl.BlockSpec((1, N_HEADS, tm, 256), lambda b, i: (b, 0, i, 0)),
                                  tok(256), pl.BlockSpec((1, 256, tm), lambda b, i: (b, 0, i))]
    return pl.pallas_call(
        functools.partial(_inproj_kernel, rope),
        out_shape=out_shape, grid=(B, n // tm), in_specs=in_specs, out_specs=out_specs,
        compiler_params=_params(("arbitrary", "arbitrary")),
        name="inproj_rope" if rope else "inproj_ctx",
    )(*args)


def _pool_kernel(n, tp, u_ref, prev_ref, next_ref, w_ref, sc_ref, o_ref, ext_ref):
    i = pl.program_id(1)
    last = pl.num_programs(1) - 1
    ext_ref[0:POOL_HALO, :] = jnp.where(i > 0, prev_ref[0], 0.0)
    ext_ref[POOL_HALO:POOL_HALO + tp, :] = u_ref[0]
    ext_ref[POOL_HALO + tp:, :] = jnp.where(i < last, next_ref[0], 0.0)

    t = i * tp + lax.broadcasted_iota(jnp.int32, (tp, LANES), 0)
    lane = lax.broadcasted_iota(jnp.int32, (tp, LANES), 1)

    def window_mean(col, w):
        acc = None
        for k in range(-(w // 2), w - w // 2):
            v = ext_ref[POOL_HALO + k:POOL_HALO + k + tp, col * LANES:(col + 1) * LANES]
            acc = v if acc is None else acc + v
        lo = jnp.clip(t - w // 2, 0, n)
        hi = jnp.clip(t - w // 2 + w, 0, n)
        return acc / (hi - lo).astype(F32)

    first = lane < POOL_GROUP_DIM
    m01 = jnp.where(first, window_mean(0, POOL_WINDOWS[0]), window_mean(0, POOL_WINDOWS[1]))
    m23 = jnp.where(first, window_mean(1, POOL_WINDOWS[2]), window_mean(1, POOL_WINDOWS[3]))
    y = (jnp.concatenate([m01, m23], axis=1) - u_ref[0]).astype(BF16)
    o_ref[0] = (_dot(y, w_ref[...]) * sc_ref[...]).astype(BF16)


def _pool(u, w_bd, scale, tp):
    B, n, W = u.shape
    hb = tp // POOL_HALO
    nh = n // POOL_HALO
    return pl.pallas_call(
        functools.partial(_pool_kernel, n, tp),
        out_shape=jax.ShapeDtypeStruct((B, n, W), BF16),
        grid=(B, n // tp),
        in_specs=[pl.BlockSpec((1, tp, W), lambda b, i: (b, i, 0)),
                  pl.BlockSpec((1, POOL_HALO, W), lambda b, i: (b, jnp.maximum(i * hb - 1, 0), 0)),
                  pl.BlockSpec((1, POOL_HALO, W), lambda b, i: (b, jnp.minimum((i + 1) * hb, nh - 1), 0)),
                  _resident(w_bd.shape), _resident(scale.shape)],
        out_specs=pl.BlockSpec((1, tp, W), lambda b, i: (b, i, 0)),
        scratch_shapes=[pltpu.VMEM((tp + 2 * POOL_HALO, W), F32)],
        compiler_params=_params(("arbitrary", "arbitrary")),
        name="pool",
    )(u, u, u, w_bd, scale)


def _head_mask(shape, hd):
    lane = lax.broadcasted_iota(jnp.int32, shape, 1)
    return (lane >= hd * HEAD_DIM) & (lane < (hd + 1) * HEAD_DIM)


def _exp_fn(log2_units):
    return jnp.exp2 if log2_units else jnp.exp


def _attend_heads(q, key_parts, val_parts, fix_parts, sink_ref, log2_units=False):
    exp = _exp_fn(log2_units)
    out = jnp.zeros(q.shape, F32)
    for hd in range(N_HEADS):
        qh = jnp.where(_head_mask(q.shape, hd), q, jnp.zeros_like(q))
        scores = []
        for kp, fix in zip(key_parts, fix_parts):
            s = _dot_nt(qh, kp)
            scores.append(s if fix is None else fix(hd, s))
        m = functools.reduce(jnp.maximum, [jnp.max(s, axis=-1, keepdims=True) for s in scores])
        if sink_ref is not None:
            sink = sink_ref[hd:hd + 1, 0:1] * (LOG2_E if log2_units else 1.0)
            m = jnp.maximum(m, sink)
        probs = [exp(s - m) for s in scores]
        den = functools.reduce(jnp.add, [jnp.sum(p, axis=-1, keepdims=True) for p in probs])
        if sink_ref is not None:
            den = den + exp(sink - m)
        o = functools.reduce(jnp.add, [_dot(p.astype(BF16), vp) for p, vp in zip(probs, val_parts)])
        out = jnp.where(_head_mask(out.shape, hd), o / den, out)
    return out


def _attend_heads_t(q, key_parts, valt_parts, fix_parts, sink_ref, log2_units=False):
    exp = _exp_fn(log2_units)
    outs = []
    for hd in range(N_HEADS):
        qh = jnp.where(_head_mask(q.shape, hd), q, jnp.zeros_like(q))
        scores = []
        for kp, fix in zip(key_parts, fix_parts):
            s = _dot_nt(kp, qh)
            scores.append(s if fix is None else fix(hd, s))
        m = functools.reduce(jnp.maximum, [jnp.max(s, axis=0, keepdims=True) for s in scores])
        if sink_ref is not None:
            sink = sink_ref[hd:hd + 1, :] * (LOG2_E if log2_units else 1.0)
            m = jnp.maximum(m, sink)
        probs = [exp(s - m) for s in scores]
        den = functools.reduce(jnp.add, [jnp.sum(p, axis=0, keepdims=True) for p in probs])
        if sink_ref is not None:
            den = den + exp(sink - m)
        o = functools.reduce(jnp.add, [_dot(vt[hd * HEAD_DIM:(hd + 1) * HEAD_DIM, :], p.astype(BF16))
                                       for p, vt in zip(probs, valt_parts)])
        outs.append(o / den)
    return jnp.concatenate(outs, axis=0).T


def _na_kernel(q_ref, k0_ref, k1_ref, k2_ref, v0_ref, v1_ref, v2_ref, kc_ref, vc_ref, bias_ref, o_ref):
    T = ATT_TILE
    fixes = [(lambda hd, s, j=j: s + bias_ref[0, hd, :, j * T:(j + 1) * T]) for j in range(3)] + [None]
    out = _attend_heads(q_ref[0], [k0_ref[0], k1_ref[0], k2_ref[0], kc_ref[0]],
                        [v0_ref[0], v1_ref[0], v2_ref[0], vc_ref[0]], fixes, None)
    o_ref[0] = out.astype(BF16)


def _neighbour_specs(nt):
    T = ATT_TILE
    lo, mid, hi = (lambda i: jnp.maximum(i - 1, 0)), (lambda i: i), (lambda i: jnp.minimum(i + 1, nt - 1))
    return [pl.BlockSpec((1, T, 256), lambda b, i, f=f: (b, f(i), 0)) for f in (lo, mid, hi)]


def _na(q, k, v, kc, vc, bias):
    B, S, W = q.shape
    T = ATT_TILE
    nt = S // T
    tile = pl.BlockSpec((1, T, W), lambda b, i: (b, i, 0))
    ctx = lambda a: pl.BlockSpec((1,) + a.shape[1:], lambda b, i: (b, 0, 0))
    variant = lambda b, i: (jnp.where(i == 0, 0, jnp.where(i == nt - 1, 2, 1)), 0, 0, 0)
    return pl.pallas_call(
        _na_kernel,
        out_shape=jax.ShapeDtypeStruct((B, S, W), BF16),
        grid=(B, nt),
        in_specs=[tile] + _neighbour_specs(nt) + _neighbour_specs(nt) + [ctx(kc), ctx(vc),
                  pl.BlockSpec((1,) + bias.shape[1:], variant)],
        out_specs=tile,
        compiler_params=_params(("arbitrary", "arbitrary")),
        name="neighbourhood_attention",
    )(q, k, k, k, v, v, v, kc, vc, bias)


def _swa_kernel(S, q_ref, k0_ref, k1_ref, k2_ref, v0_ref, v1_ref, v2_ref, kc_ref, vc_ref, sink_ref, o_ref):
    T = ATT_TILE
    i = pl.program_id(1)

    def band(first_key, n_keys):
        kpos = first_key + lax.broadcasted_iota(jnp.int32, (n_keys, T), 0)
        qpos = i * T + lax.broadcasted_iota(jnp.int32, (n_keys, T), 1)
        ok = (jnp.abs(kpos - qpos) <= SWA_WINDOW) & (kpos >= 0) & (kpos < S)
        return lambda hd, s: jnp.where(ok, s, NEG_INF)

    fixes = [band(i * T - SWA_WINDOW, SWA_WINDOW), band(i * T, T), band((i + 1) * T, SWA_WINDOW), None]
    out = _attend_heads_t(q_ref[0], [k0_ref[0], k1_ref[0], k2_ref[0], kc_ref[0]],
                          [v0_ref[0], v1_ref[0], v2_ref[0], vc_ref[0]], fixes, sink_ref, log2_units=True)
    o_ref[0] = out.astype(BF16)


def _swa(q, k, vt, kc, vtc, sink_tab):
    B, S, W = q.shape
    T = ATT_TILE
    nt = S // T
    tile = pl.BlockSpec((1, T, W), lambda b, i: (b, i, 0))
    ctx = lambda a: pl.BlockSpec((1,) + a.shape[1:], lambda b, i: (b, 0, 0))
    half = T // SWA_WINDOW
    prev = lambda i: jnp.maximum(half * i - 1, 0)
    nxt = lambda i: jnp.minimum(half * (i + 1), half * nt - 1)
    keys = [pl.BlockSpec((1, SWA_WINDOW, W), lambda b, i: (b, prev(i), 0)), tile,
            pl.BlockSpec((1, SWA_WINDOW, W), lambda b, i: (b, nxt(i), 0))]
    vals = [pl.BlockSpec((1, W, SWA_WINDOW), lambda b, i: (b, 0, prev(i))),
            pl.BlockSpec((1, W, T), lambda b, i: (b, 0, i)),
            pl.BlockSpec((1, W, SWA_WINDOW), lambda b, i: (b, 0, nxt(i)))]
    return pl.pallas_call(
        functools.partial(_swa_kernel, S),
        out_shape=jax.ShapeDtypeStruct((B, S, W), BF16),
        grid=(B, nt),
        in_specs=[tile] + keys + vals + [ctx(kc), ctx(vtc), _resident(sink_tab.shape)],
        out_specs=tile,
        compiler_params=_params(("arbitrary", "arbitrary")),
        name="windowed_attention",
    )(q, k, k, k, vt, vt, vt, kc, vtc, sink_tab)


def _mla_finish(acc, wuvt_ref, tq):
    o = (acc[0:MLA_KV_RANK] / acc[MLA_KV_RANK:MLA_KV_RANK + 1]).astype(BF16)
    outs = [_dot(wuvt_ref[hd], o[:, hd * tq:(hd + 1) * tq]) for hd in range(N_HEADS)]
    return jnp.concatenate(outs, axis=0).T


def _mla_kernel(n_chunks, unroll, tk, q_ref, k_ref, vt_ref, kc_ref, vtc_ref, wuvt_ref, o_ref,
                m_ref, ex_ref, acc_ref):
    tq = q_ref.shape[2]
    q = q_ref[0].reshape(N_HEADS * tq, q_ref.shape[3])

    def scores(c):
        off = pl.multiple_of(c * tk, tk)
        return _dot_nt(k_ref[0, pl.ds(off, tk), :], q)

    def values(c):
        return vt_ref[0, :, pl.ds(pl.multiple_of(c * tk, tk), tk)]

    def start():
        s = _dot_nt(kc_ref[0], q)
        m = jnp.max(s, axis=0, keepdims=True)
        m_ref[...] = m
        acc_ref[...] = _dot(vtc_ref[0], jnp.exp2(s - m).astype(BF16))

    def lagged(s, vt):
        m_old = m_ref[...]
        p = jnp.exp2(s - m_old).astype(BF16)
        cm = jnp.max(s, axis=0, keepdims=True)
        m_new = jnp.maximum(m_old, cm)
        ex_ref[...] = jnp.maximum(ex_ref[...], cm - m_old)
        m_ref[...] = m_new
        acc_ref[...] = (acc_ref[...] + _dot(vt, p)) * jnp.exp2(m_old - m_new)

    def exact(c, carry):
        s = scores(c)
        m_old = m_ref[...]
        m_new = jnp.maximum(m_old, jnp.max(s, axis=0, keepdims=True))
        p = jnp.exp2(s - m_new).astype(BF16)
        m_ref[...] = m_new
        acc_ref[...] = jnp.exp2(m_old - m_new) * acc_ref[...] + _dot(values(c), p)
        return carry

    def group(cc, carry):
        m_old = m_ref[...]
        pv, cm = None, None
        for j in range(unroll):
            c = unroll * cc + j
            s = scores(c)
            p = jnp.exp2(s - m_old).astype(BF16)
            cj = jnp.max(s, axis=0, keepdims=True)
            cm = cj if cm is None else jnp.maximum(cm, cj)
            d = _dot(values(c), p)
            pv = d if pv is None else pv + d
        m_new = jnp.maximum(m_old, cm)
        ex_ref[...] = jnp.maximum(ex_ref[...], cm - m_old)
        m_ref[...] = m_new
        acc_ref[...] = (acc_ref[...] + pv) * jnp.exp2(m_old - m_new)
        return carry

    start()
    ex_ref[...] = jnp.zeros(ex_ref.shape, F32)
    lax.fori_loop(0, n_chunks // unroll, group, 0)

    @pl.when(jnp.max(ex_ref[...]) > MLA_LAG_LIMIT)
    def _():
        start()
        lax.fori_loop(0, n_chunks, exact, 0)

    o_ref[0] = _mla_finish(acc_ref[...], wuvt_ref, tq).astype(BF16)


def _mla(q, k, vt, kc, vtc, wuvt):
    B, H, S, W = q.shape
    tq, tk = MLA_TQ, MLA_TK
    n_chunks = S // tk
    assert S % tk == 0
    unroll = max(u for u in range(1, MLA_UNROLL + 1) if n_chunks % u == 0)
    per_batch = lambda shape: pl.BlockSpec((1,) + shape[1:], lambda b, i: (b,) + (0,) * (len(shape) - 1))
    return pl.pallas_call(
        functools.partial(_mla_kernel, n_chunks, unroll, tk),
        out_shape=jax.ShapeDtypeStruct((B, S, BRANCH_W), BF16),
        grid=(B, S // tq),
        in_specs=[pl.BlockSpec((1, H, tq, W), lambda b, i: (b, 0, i, 0)),
                  per_batch(k.shape), per_batch(vt.shape), per_batch(kc.shape), per_batch(vtc.shape),
                  _resident(wuvt.shape)],
        out_specs=pl.BlockSpec((1, tq, BRANCH_W), lambda b, i: (b, i, 0)),
        scratch_shapes=[pltpu.VMEM((1, H * tq), F32), pltpu.VMEM((1, H * tq), F32),
                        pltpu.VMEM((vt.shape[1], H * tq), F32)],
        compiler_params=_params(("arbitrary", "arbitrary")),
        name="latent_attention",
    )(q, k, vt, kc, vtc, wuvt)


def _ctx_attn_kernel(naq_ref, nak_ref, nav_ref, swq_ref, swk_ref, swv_ref, qmla_ref, kcat_ref, ckvt_ref,
                     sink_ref, wuvt_ref, na_o_ref, sw_o_ref, mla_o_ref):
    na_o_ref[0] = _attend_heads(naq_ref[0], [nak_ref[0]], [nav_ref[0]], [None], None).astype(BF16)
    sw_o_ref[0] = _attend_heads(swq_ref[0], [swk_ref[0]], [swv_ref[0]], [None], sink_ref,
                                log2_units=True).astype(BF16)
    tq = qmla_ref.shape[2]
    q = qmla_ref[0].reshape(N_HEADS * tq, qmla_ref.shape[3])
    s = _dot_nt(kcat_ref[0], q)
    p = jnp.exp2(s - jnp.max(s, axis=0, keepdims=True))
    acc = _dot(ckvt_ref[0], p.astype(BF16))
    mla_o_ref[0] = _mla_finish(acc, wuvt_ref, tq).astype(BF16)


def _ctx_attn(naq, nak, nav, swq, swk, swv, qmla, kcat, ckvt, sink_tab, wuvt):
    B, Lc, W = naq.shape
    per_batch = lambda a: pl.BlockSpec((1,) + a.shape[1:], lambda b: (b,) + (0,) * (a.ndim - 1))
    acts = [naq, nak, nav, swq, swk, swv, qmla, kcat, ckvt]
    out = jax.ShapeDtypeStruct((B, Lc, W), BF16)
    return pl.pallas_call(
        _ctx_attn_kernel,
        out_shape=[out, out, out],
        grid=(B,),
        in_specs=[per_batch(a) for a in acts] + [_resident(sink_tab.shape), _resident(wuvt.shape)],
        out_specs=[per_batch(naq)] * 3,
        compiler_params=_params(("arbitrary",)),
        name="context_attention",
    )(*acts, sink_tab, wuvt)


def _merge_kernel(x_ref, mod_ref, g_ref, y0_ref, y1_ref, y2_ref, y3_ref, wg_ref, wb_ref, wo_ref, o_ref):
    x = x_ref[0]
    D = x.shape[1]
    h = _modulated_norm(x, g_ref[...], mod_ref[0, 0:1, :], mod_ref[0, 1:2, :]).astype(BF16)
    merged = None
    for i, y_ref in enumerate((y0_ref, y1_ref, y2_ref, y3_ref)):
        gate = jax.nn.sigmoid(_dot(h, wg_ref[:, i * D:(i + 1) * D]))
        term = gate * _dot(y_ref[0], wb_ref[i])
        merged = term if merged is None else merged + term
    o_ref[0] = x + mod_ref[0, 2:3, :] * _dot(merged.astype(BF16), wo_ref[...])


def _mod_spec(mod, D):
    return pl.BlockSpec((1, 6, D), (lambda b, i: (b, 0, 0)) if mod.shape[0] > 1 else (lambda b, i: (0, 0, 0)))


def _merge(x, mod, g, ys, wg, wb, wo, tm):
    B, n, D = x.shape
    tok = lambda w: pl.BlockSpec((1, tm, w), lambda b, i: (b, i, 0))
    return pl.pallas_call(
        _merge_kernel,
        out_shape=jax.ShapeDtypeStruct((B, n, D), F32),
        grid=(B, n // tm),
        in_specs=[tok(D), _mod_spec(mod, D), _resident(g.shape)] + [tok(BRANCH_W)] * 4
                 + [_resident(wg.shape), _resident(wb.shape), _resident(wo.shape)],
        out_specs=tok(D),
        compiler_params=_params(("arbitrary", "arbitrary")),
        name="merge",
    )(x, mod, g, *ys, wg, wb, wo)


def _ffn_kernel(final, x_ref, mod_ref, g_ref, w1_ref, w3_ref, w2_ref, *rest):
    if final:
        fg_ref, o_ref = rest
    else:
        (o_ref,) = rest
    x = x_ref[0]
    h = _modulated_norm(x, g_ref[...], mod_ref[0, 3:4, :], mod_ref[0, 4:5, :]).astype(BF16)
    a = _dot(h, w1_ref[...])
    act = ((a * jax.nn.sigmoid(a)) * _dot(h, w3_ref[...])).astype(BF16)
    y = x + mod_ref[0, 5:6, :] * _dot(act, w2_ref[...])
    o_ref[0] = _rms(y, fg_ref[...]) if final else y


def _ffn(x, mod, g, w1, w3, w2, final_g, tm):
    B, n, D = x.shape
    final = final_g is not None
    tok = pl.BlockSpec((1, tm, D), lambda b, i: (b, i, 0))
    in_specs = [tok, _mod_spec(mod, D), _resident(g.shape), _resident(w1.shape), _resident(w3.shape),
                _resident(w2.shape)]
    args = [x, mod, g, w1, w3, w2]
    if final:
        in_specs.append(_resident(final_g.shape))
        args.append(final_g)
    return pl.pallas_call(
        functools.partial(_ffn_kernel, final),
        out_shape=jax.ShapeDtypeStruct((B, n, D), F32),
        grid=(B, n // tm),
        in_specs=in_specs, out_specs=tok,
        compiler_params=_params(("arbitrary", "arbitrary")),
        name="ffn_final" if final else "ffn",
    )(*args)


def _pack_w_in(w):
    def dup_heads(a):
        return jnp.concatenate([a[:, :64], a[:, :64], a[:, 64:], a[:, 64:]], axis=1)

    w = w.astype(BF16)
    kr = w[:, OFF_MLA_KR:KV_COLS]
    cols = [w[:, OFF_NA_Q:OFF_NA_Q + 256], w[:, OFF_NA_K:OFF_NA_K + 256], w[:, OFF_NA_V:OFF_NA_V + 256],
            w[:, OFF_SWA_Q:OFF_SWA_Q + 256], dup_heads(w[:, OFF_SWA_K:OFF_SWA_V]),
            dup_heads(w[:, OFF_SWA_V:OFF_MLA_CKV]), w[:, OFF_MLA_CKV:OFF_MLA_KR],
            jnp.concatenate([kr] * N_HEADS, axis=1), w[:, OFF_MLA_CQ:OFF_POOL], w[:, OFF_POOL:OFF_GATE]]
    return jnp.concatenate(cols, axis=1), w[:, OFF_GATE:]


def _block_diag(blocks):
    r, c = blocks[0].shape
    n = len(blocks)
    rows = [jnp.concatenate([blocks[i] if j == i else jnp.zeros((r, c), blocks[0].dtype) for j in range(n)], axis=1)
            for i in range(n)]
    return jnp.concatenate(rows, axis=0)


def _rope_tables(n, dim):
    rows = n // GRID_W
    n_freq = dim // 4
    inv = jnp.power(ROPE_BASE, -jnp.arange(n_freq, dtype=F32) / n_freq)
    reps = LANES // dim
    zeros = lambda a: jnp.zeros_like(a)

    def lanes(first, second, sign):
        head = jnp.concatenate([sign * first, sign * second, first, second], axis=1)
        return jnp.tile(head, (1, reps))

    out = []
    for count, is_row in ((rows, True), (GRID_W, False)):
        ang = jnp.arange(count, dtype=jnp.int32).astype(F32)[:, None] * inv
        cos, sin = jnp.cos(ang), jnp.sin(ang)
        if is_row:
            out += [lanes(cos, zeros(cos), 1.0), lanes(sin, zeros(sin), -1.0)]
        else:
            out += [lanes(zeros(cos), cos, 1.0), lanes(zeros(sin), sin, -1.0)]
    return tuple(out)


def _na_bias_tables(rpb, rows):
    T = ATT_TILE
    tr = T // GRID_W
    nt = rows // tr
    L, H = rpb.shape[:2]
    n_off = 2 * NA_ROWS - 1
    assert 3 * tr + tr - 1 == n_off
    pad = GRID_W - NA_COLS
    padded = jnp.pad(rpb.astype(F32), ((0, 0), (0, 0), (0, 0), (pad, pad)))
    toeplitz = jnp.stack([padded[..., GRID_W - 1 - qc:2 * GRID_W - 1 - qc] for qc in range(GRID_W)], axis=-2)
    qc, kc = np.arange(GRID_W)[:, None], np.arange(GRID_W)[None, :]
    c0 = np.clip(qc - NA_COLS // 2, 0, GRID_W - NA_COLS)
    col_ok = (kc >= c0) & (kc < c0 + NA_COLS)
    blocks = jnp.where(jnp.asarray(col_ok), toeplitz, NEG_INF)
    tabs = []
    for i in (0, 1, nt - 1):
        per_row = []
        for ql in range(tr):
            qr = tr * i + ql
            r0 = min(max(qr - NA_ROWS // 2, 0), rows - NA_ROWS)
            kr = tr * (i - 1) + np.arange(3 * tr)
            row_ok = (kr >= r0) & (kr < r0 + NA_ROWS)
            shifted = blocks[:, :, tr - 1 - ql:tr - 1 - ql + 3 * tr]
            per_row.append(jnp.where(jnp.asarray(row_ok)[:, None, None], shifted, NEG_INF))
        tab = jnp.stack(per_row, axis=2)
        tabs.append(jnp.transpose(tab, (0, 1, 2, 4, 3, 5)).reshape(L, H, T, 3 * T))
    return jnp.stack(tabs, axis=1)


def kernel(x, c, ctx, c_ctx, ada_w, ada_b, norm1_g, norm2_g, w_in, pool_w, pool_scale, na_rpb, swa_sink,
           mla_q_norm, mla_kv_norm, mla_w_uq, mla_w_uk, mla_w_uv, w_branch, w_out, ffn_w1, ffn_w3, ffn_w2,
           final_norm_g):
    B, S, D = x.shape
    Lc = ctx.shape[1]
    depth = ada_w.shape[0]
    rows = S // GRID_W
    assert S % 1024 == 0 and rows >= 3 * (ATT_TILE // GRID_W) and Lc == ATT_TILE and B <= 7
    tm_x, tm_c = 512, Lc

    cond = jnp.concatenate([c, c_ctx[None], jnp.zeros((8 - B - 1, D), F32)], axis=0)
    mods = _modulation(cond, ada_w, ada_b[:, None, :])
    rope_tabs = _rope_tables(S, HEAD_DIM) + _rope_tables(S, MLA_ROPE)
    bias_tabs = _na_bias_tables(na_rpb, rows)
    row2 = lambda v: v.reshape(1, -1)

    xc = ctx
    for l in range(depth):
        last = l == depth - 1
        mod_x = mods[l, :B].reshape(B, 6, D)
        mod_c = mods[l, B:B + 1].reshape(1, 6, D)
        wp, wg = _pack_w_in(w_in[l])
        uq = mla_w_uq[l].reshape(-1, N_HEADS, MLA_NOPE + MLA_ROPE)
        wuq = jnp.concatenate([uq[:, :, :MLA_NOPE].reshape(-1, N_HEADS * MLA_NOPE),
                               uq[:, :, MLA_NOPE:].reshape(-1, N_HEADS * MLA_ROPE)], axis=1).astype(BF16)
        uk = jnp.transpose(mla_w_uk[l], (1, 2, 0))
        wuk = _block_diag([uk[h] for h in range(N_HEADS)]).astype(BF16)
        wuvt = jnp.transpose(mla_w_uv[l], (1, 2, 0)).astype(BF16)
        w_pool = _block_diag([pool_w[l, g] for g in range(len(POOL_WINDOWS))]).astype(BF16)
        sink_tab = jnp.concatenate([jnp.broadcast_to(swa_sink[l][:, None], (N_HEADS, ATT_TILE)),
                                    jnp.zeros((8 - N_HEADS, ATT_TILE), F32)], axis=0)
        bias = bias_tabs[l]
        g1, g2 = row2(norm1_g[l]), row2(norm2_g[l])
        qg, kvg, psc = row2(mla_q_norm[l]), row2(mla_kv_norm[l]), row2(pool_scale[l])
        wb, wo = w_branch[l].astype(BF16), w_out[l].astype(BF16)
        w1, w3, w2 = ffn_w1[l].astype(BF16), ffn_w3[l].astype(BF16), ffn_w2[l].astype(BF16)

        (naq, nak, nav, swq, swk, _, kcat, ckvt, qmla, pool_in, swvt) = _inproj(
            x, mod_x, g1, wp, wuq, wuk, qg, kvg, rope_tabs, tm_x)
        (c_naq, c_nak, c_nav, c_swq, c_swk, c_swv, c_kcat, c_ckvt, c_qmla, c_pool_in, c_swvt) = _inproj(
            xc, mod_c, g1, wp, wuq, wuk, qg, kvg, None, tm_c)

        ys = [_pool(pool_in, w_pool, psc, 512),
              _na(naq, nak, nav, c_nak, c_nav, bias),
              _swa(swq, swk, swvt, c_swk, c_swvt, sink_tab),
              _mla(qmla, kcat, ckvt, c_kcat, c_ckvt, wuvt)]
        if not last:
            ys_c = [_pool(c_pool_in, w_pool, psc, Lc)] + list(_ctx_attn(
                c_naq, c_nak, c_nav, c_swq, c_swk, c_swv, c_qmla, c_kcat, c_ckvt, sink_tab, wuvt))
            xc = _merge(xc, mod_c, g1, ys_c, wg, wb, wo, tm_c)
            xc = _ffn(xc, mod_c, g2, w1, w3, w2, None, tm_c)
        x = _merge(x, mod_x, g1, ys, wg, wb, wo, tm_x)
        x = _ffn(x, mod_x, g2, w1, w3, w2, row2(final_norm_g) if last else None, tm_x)
    return x
```

```python
import functools

import numpy as np
import jax
import jax.numpy as jnp
from jax import lax
from jax.experimental import pallas as pl
from jax.experimental.pallas import tpu as pltpu

F32 = jnp.float32
BF16 = jnp.bfloat16

GRID_W = 64
HEAD_DIM = 64
ROPE_BASE = 10000.0
NEG_INF = -1e30
EPS = 1e-6
POOL_WINDOWS = (2, 4, 8, 16)
POOL_GROUP_DIM = 64
N_HEADS = 4
NA_ROWS = 8
NA_COLS = 16
SWA_WINDOW = 128
MLA_KV_RANK = 128
MLA_NOPE = 64
MLA_ROPE = 32
MLA_SCALE = (MLA_NOPE + MLA_ROPE) ** -0.5
LOG2_E = 1.4426950408889634
MLA_Q_SCALE = MLA_SCALE * LOG2_E
MLA_SUM_ROWS = 16
MLA_LAG_LIMIT = 64.0
ATT_SCALE = HEAD_DIM ** -0.5
BRANCH_W = 256

OFF_NA_K, OFF_NA_V, OFF_SWA_K, OFF_SWA_V, OFF_MLA_CKV, OFF_MLA_KR = 0, 256, 512, 640, 768, 896
KV_COLS = 928
OFF_NA_Q, OFF_SWA_Q, OFF_MLA_CQ, OFF_POOL, OFF_GATE = 928, 1184, 1440, 1696, 1952

LANES = 128
ATT_TILE = 256
MLA_TQ = 256
MLA_TK = 512
MLA_UNROLL = 32
POOL_HALO = 8
VMEM_LIMIT = 56 * 1024 * 1024


def _params(sem):
    return pltpu.CompilerParams(dimension_semantics=sem, vmem_limit_bytes=VMEM_LIMIT)


def _dot(a, b):
    return jnp.dot(a, b, preferred_element_type=F32)


def _dot_nt(a, b):
    return lax.dot_general(a, b, (((1,), (1,)), ((), ())), preferred_element_type=F32)


def _resident(shape):
    n = len(shape)
    return pl.BlockSpec(shape, lambda *_: (0,) * n, pipeline_mode=pl.Buffered(1))


def _modulated_norm(x, g, shift, scale):
    y = x * lax.rsqrt(jnp.mean(x * x, axis=-1, keepdims=True) + EPS)
    return (y * g) * (1.0 + scale) + shift


def _rms(x, g):
    return x * lax.rsqrt(jnp.mean(x * x, axis=-1, keepdims=True) + EPS) * g


def _rope_lanes(t, cos, sin_signed, half):
    lane = lax.broadcasted_iota(jnp.int32, t.shape, 1)
    up = pltpu.roll(t, LANES - half, 1)
    down = pltpu.roll(t, half, 1)
    partner = jnp.where((lane & (2 * half - 1)) < half, up, down)
    return t * cos + partner * sin_signed


def _mod_kernel(c_ref, w_ref, b_ref, o_ref):
    c = c_ref[...]
    s = (c * jax.nn.sigmoid(c)).astype(BF16)
    o_ref[0] = _dot(s, w_ref[0].astype(BF16)) + b_ref[0]


def _modulation(cond, ada_w, ada_b):
    L, D, N = ada_w.shape
    tn = N // 4
    return pl.pallas_call(
        _mod_kernel,
        out_shape=jax.ShapeDtypeStruct((L, cond.shape[0], N), F32),
        grid=(L, N // tn),
        in_specs=[pl.BlockSpec(cond.shape, lambda l, j: (0, 0)),
                  pl.BlockSpec((1, D, tn), lambda l, j: (l, 0, j)),
                  pl.BlockSpec((1, 1, tn), lambda l, j: (l, 0, j))],
        out_specs=pl.BlockSpec((1, cond.shape[0], tn), lambda l, j: (l, 0, j)),
        compiler_params=_params(("arbitrary", "arbitrary")),
        name="modulation",
    )(cond, ada_w, ada_b)


def _inproj_kernel(rope, x_ref, mod_ref, g_ref, w_ref, wuq_ref, wuk_ref, qg_ref, kvg_ref, *rest):
    if rope:
        tm = x_ref.shape[1]
        g = tm // GRID_W

        def per_token(row_ref, col_ref):
            r = jnp.broadcast_to(row_ref[...][:, None, :], (g, GRID_W, LANES)).reshape(tm, LANES)
            c = jnp.broadcast_to(col_ref[...][None, :, :], (g, GRID_W, LANES)).reshape(tm, LANES)
            return r + c

        c64, s64 = per_token(rest[0], rest[2]), per_token(rest[1], rest[3])
        c32, s32 = per_token(rest[4], rest[6]), per_token(rest[5], rest[7])
        rest = rest[8:]
    (naq_ref, nak_ref, nav_ref, swq_ref, swk_ref, swv_ref, kcat_ref, ckvt_ref, qmla_ref, pool_ref,
     swvt_ref) = rest

    x = x_ref[0]
    h = _modulated_norm(x, g_ref[...], mod_ref[0, 0:1, :], mod_ref[0, 1:2, :]).astype(BF16)

    def seg(a, n):
        return _dot(h, w_ref[:, a:a + n])

    def rope64(t):
        if not rope:
            return t
        return jnp.concatenate([_rope_lanes(t[:, :LANES], c64, s64, 32),
                                _rope_lanes(t[:, LANES:], c64, s64, 32)], axis=1)

    def rope32(t):
        return _rope_lanes(t, c32, s32, 16) if rope else t

    naq_ref[0] = (seg(0, 256) * ATT_SCALE).astype(BF16)
    nak_ref[0] = seg(256, 256).astype(BF16)
    nav_ref[0] = seg(512, 256).astype(BF16)
    swq_ref[0] = (rope64(seg(768, 256)) * (ATT_SCALE * LOG2_E)).astype(BF16)
    swk_ref[0] = rope64(seg(1024, 256)).astype(BF16)
    swv = seg(1280, 256)
    swv_ref[0] = swv.astype(BF16)
    swvt_ref[0] = swv.T.astype(BF16)

    ckv = _rms(seg(1536, 128), kvg_ref[...])
    kcat_ref[0, :, 0:LANES] = ckv.astype(BF16)
    kcat_ref[0, :, LANES:2 * LANES] = rope32(seg(1664, 128)).astype(BF16)
    ckvt_ref[0, 0:MLA_KV_RANK, :] = ckv.T.astype(BF16)
    ckvt_ref[0, MLA_KV_RANK:, :] = jnp.ones((MLA_SUM_ROWS, ckv.shape[0]), BF16)

    cq = _rms(seg(1792, 256), qg_ref[...]).astype(BF16)
    qq = _dot(cq, wuq_ref[...])
    q_lat = _dot(qq[:, :256].astype(BF16), wuk_ref[...])
    q_rope = rope32(qq[:, 256:384])
    lane = lax.broadcasted_iota(jnp.int32, q_rope.shape, 1)
    for hd in range(N_HEADS):
        qmla_ref[0, hd, :, 0:LANES] = (q_lat[:, hd * LANES:(hd + 1) * LANES] * MLA_Q_SCALE).astype(BF16)
        own = (lane >= hd * MLA_ROPE) & (lane < (hd + 1) * MLA_ROPE)
        qmla_ref[0, hd, :, LANES:2 * LANES] = (jnp.where(own, q_rope, 0.0) * MLA_Q_SCALE).astype(BF16)

    pool_ref[0] = seg(2048, 256)


def _inproj(x, mod, g, wp, wuq, wuk, qg, kvg, rope_tabs, tm):
    B, n, D = x.shape
    rope = rope_tabs is not None
    mod_b = mod.shape[0]
    tok = lambda w: pl.BlockSpec((1, tm, w), lambda b, i: (b, i, 0))
    in_specs = [tok(D),
                pl.BlockSpec((1, 6, D), (lambda b, i: (b, 0, 0)) if mod_b > 1 else (lambda b, i: (0, 0, 0))),
                _resident(g.shape), _resident(wp.shape), _resident(wuq.shape), _resident(wuk.shape),
                _resident(qg.shape), _resident(kvg.shape)]
    args = [x, mod, g, wp, wuq, wuk, qg, kvg]
    if rope:
        assert tm % (8 * GRID_W) == 0
        row_tab = pl.BlockSpec((tm // GRID_W, LANES), lambda b, i: (i, 0))
        col_tab = _resident((GRID_W, LANES))
        in_specs += [row_tab, row_tab, col_tab, col_tab] * 2
        args += list(rope_tabs)
    bf = lambda w: jax.ShapeDtypeStruct((B, n, w), BF16)
    vt_rows = MLA_KV_RANK + MLA_SUM_ROWS
    out_shape = [bf(256)] * 7 + [jax.ShapeDtypeStruct((B, vt_rows, n), BF16),
                                 jax.ShapeDtypeStruct((B, N_HEADS, n, 256), BF16),
                                 jax.ShapeDtypeStruct((B, n, 256), F32), jax.ShapeDtypeStruct((B, 256, n), BF16)]
    out_specs = [tok(256)] * 7 + [pl.BlockSpec((1, vt_rows, tm), lambda b, i: (b, 0, i)),
                                  pl.BlockSpec((1, N_HEADS, tm, 256), lambda b, i: (b, 0, i, 0)),
                                  tok(256), pl.BlockSpec((1, 256, tm), lambda b, i: (b, 0, i))]
    return pl.pallas_call(
        functools.partial(_inproj_kernel, rope),
        out_shape=out_shape, grid=(B, n // tm), in_specs=in_specs, out_specs=out_specs,
        compiler_params=_params(("arbitrary", "arbitrary")),
        name="inproj_rope" if rope else "inproj_ctx",
    )(*args)


def _pool_kernel(n, tp, u_ref, prev_ref, next_ref, w_ref, sc_ref, o_ref, ext_ref):
    i = pl.program_id(1)
    last = pl.num_programs(1) - 1
    ext_ref[0:POOL_HALO, :] = jnp.where(i > 0, prev_ref[0], 0.0)
    ext_ref[POOL_HALO:POOL_HALO + tp, :] = u_ref[0]
    ext_ref[POOL_HALO + tp:, :] = jnp.where(i < last, next_ref[0], 0.0)

    t = i * tp + lax.broadcasted_iota(jnp.int32, (tp, LANES), 0)
    lane = lax.broadcasted_iota(jnp.int32, (tp, LANES), 1)

    def window_mean(col, w):
        acc = None
        for k in range(-(w // 2), w - w // 2):
            v = ext_ref[POOL_HALO + k:POOL_HALO + k + tp, col * LANES:(col + 1) * LANES]
            acc = v if acc is None else acc + v
        lo = jnp.clip(t - w // 2, 0, n)
        hi = jnp.clip(t - w // 2 + w, 0, n)
        return acc / (hi - lo).astype(F32)

    first = lane < POOL_GROUP_DIM
    m01 = jnp.where(first, window_mean(0, POOL_WINDOWS[0]), window_mean(0, POOL_WINDOWS[1]))
    m23 = jnp.where(first, window_mean(1, POOL_WINDOWS[2]), window_mean(1, POOL_WINDOWS[3]))
    y = (jnp.concatenate([m01, m23], axis=1) - u_ref[0]).astype(BF16)
    o_ref[0] = (_dot(y, w_ref[...]) * sc_ref[...]).astype(BF16)


def _pool(u, w_bd, scale, tp):
    B, n, W = u.shape
    hb = tp // POOL_HALO
    nh = n // POOL_HALO
    return pl.pallas_call(
        functools.partial(_pool_kernel, n, tp),
        out_shape=jax.ShapeDtypeStruct((B, n, W), BF16),
        grid=(B, n // tp),
        in_specs=[pl.BlockSpec((1, tp, W), lambda b, i: (b, i, 0)),
                  pl.BlockSpec((1, POOL_HALO, W), lambda b, i: (b, jnp.maximum(i * hb - 1, 0), 0)),
                  pl.BlockSpec((1, POOL_HALO, W), lambda b, i: (b, jnp.minimum((i + 1) * hb, nh - 1), 0)),
                  _resident(w_bd.shape), _resident(scale.shape)],
        out_specs=pl.BlockSpec((1, tp, W), lambda b, i: (b, i, 0)),
        scratch_shapes=[pltpu.VMEM((tp + 2 * POOL_HALO, W), F32)],
        compiler_params=_params(("arbitrary", "arbitrary")),
        name="pool",
    )(u, u, u, w_bd, scale)


def _head_mask(shape, hd):
    lane = lax.broadcasted_iota(jnp.int32, shape, 1)
    return (lane >= hd * HEAD_DIM) & (lane < (hd + 1) * HEAD_DIM)


def _exp_fn(log2_units):
    return jnp.exp2 if log2_units else jnp.exp


def _attend_heads(q, key_parts, val_parts, fix_parts, sink_ref, log2_units=False):
    exp = _exp_fn(log2_units)
    out = jnp.zeros(q.shape, F32)
    for hd in range(N_HEADS):
        qh = jnp.where(_head_mask(q.shape, hd), q, jnp.zeros_like(q))
        scores = []
        for kp, fix in zip(key_parts, fix_parts):
            s = _dot_nt(qh, kp)
            scores.append(s if fix is None else fix(hd, s))
        m = functools.reduce(jnp.maximum, [jnp.max(s, axis=-1, keepdims=True) for s in scores])
        if sink_ref is not None:
            sink = sink_ref[hd:hd + 1, 0:1] * (LOG2_E if log2_units else 1.0)
            m = jnp.maximum(m, sink)
        probs = [exp(s - m) for s in scores]
        den = functools.reduce(jnp.add, [jnp.sum(p, axis=-1, keepdims=True) for p in probs])
        if sink_ref is not None:
            den = den + exp(sink - m)
        o = functools.reduce(jnp.add, [_dot(p.astype(BF16), vp) for p, vp in zip(probs, val_parts)])
        out = jnp.where(_head_mask(out.shape, hd), o / den, out)
    return out


def _attend_heads_t(q, key_parts, valt_parts, fix_parts, sink_ref, log2_units=False):
    exp = _exp_fn(log2_units)
    outs = []
    for hd in range(N_HEADS):
        qh = jnp.where(_head_mask(q.shape, hd), q, jnp.zeros_like(q))
        scores = []
        for kp, fix in zip(key_parts, fix_parts):
            s = _dot_nt(kp, qh)
            scores.append(s if fix is None else fix(hd, s))
        m = functools.reduce(jnp.maximum, [jnp.max(s, axis=0, keepdims=True) for s in scores])
        if sink_ref is not None:
            sink = sink_ref[hd:hd + 1, :] * (LOG2_E if log2_units else 1.0)
            m = jnp.maximum(m, sink)
        probs = [exp(s - m) for s in scores]
        den = functools.reduce(jnp.add, [jnp.sum(p, axis=0, keepdims=True) for p in probs])
        if sink_ref is not None:
            den = den + exp(sink - m)
        o = functools.reduce(jnp.add, [_dot(vt[hd * HEAD_DIM:(hd + 1) * HEAD_DIM, :], p.astype(BF16))
                                       for p, vt in zip(probs, valt_parts)])
        outs.append(o / den)
    return jnp.concatenate(outs, axis=0).T


def _na_kernel(q_ref, k0_ref, k1_ref, k2_ref, v0_ref, v1_ref, v2_ref, kc_ref, vc_ref, bias_ref, o_ref):
    T = ATT_TILE
    fixes = [(lambda hd, s, j=j: s + bias_ref[0, hd, :, j * T:(j + 1) * T]) for j in range(3)] + [None]
    out = _attend_heads(q_ref[0], [k0_ref[0], k1_ref[0], k2_ref[0], kc_ref[0]],
                        [v0_ref[0], v1_ref[0], v2_ref[0], vc_ref[0]], fixes, None)
    o_ref[0] = out.astype(BF16)


def _neighbour_specs(nt):
    T = ATT_TILE
    lo, mid, hi = (lambda i: jnp.maximum(i - 1, 0)), (lambda i: i), (lambda i: jnp.minimum(i + 1, nt - 1))
    return [pl.BlockSpec((1, T, 256), lambda b, i, f=f: (b, f(i), 0)) for f in (lo, mid, hi)]


def _na(q, k, v, kc, vc, bias):
    B, S, W = q.shape
    T = ATT_TILE
    nt = S // T
    tile = pl.BlockSpec((1, T, W), lambda b, i: (b, i, 0))
    ctx = lambda a: pl.BlockSpec((1,) + a.shape[1:], lambda b, i: (b, 0, 0))
    variant = lambda b, i: (jnp.where(i == 0, 0, jnp.where(i == nt - 1, 2, 1)), 0, 0, 0)
    return pl.pallas_call(
        _na_kernel,
        out_shape=jax.ShapeDtypeStruct((B, S, W), BF16),
        grid=(B, nt),
        in_specs=[tile] + _neighbour_specs(nt) + _neighbour_specs(nt) + [ctx(kc), ctx(vc),
                  pl.BlockSpec((1,) + bias.shape[1:], variant)],
        out_specs=tile,
        compiler_params=_params(("arbitrary", "arbitrary")),
        name="neighbourhood_attention",
    )(q, k, k, k, v, v, v, kc, vc, bias)


def _swa_kernel(S, q_ref, k0_ref, k1_ref, k2_ref, v0_ref, v1_ref, v2_ref, kc_ref, vc_ref, sink_ref, o_ref):
    T = ATT_TILE
    i = pl.program_id(1)

    def band(first_key, n_keys):
        kpos = first_key + lax.broadcasted_iota(jnp.int32, (n_keys, T), 0)
        qpos = i * T + lax.broadcasted_iota(jnp.int32, (n_keys, T), 1)
        ok = (jnp.abs(kpos - qpos) <= SWA_WINDOW) & (kpos >= 0) & (kpos < S)
        return lambda hd, s: jnp.where(ok, s, NEG_INF)

    fixes = [band(i * T - SWA_WINDOW, SWA_WINDOW), band(i * T, T), band((i + 1) * T, SWA_WINDOW), None]
    out = _attend_heads_t(q_ref[0], [k0_ref[0], k1_ref[0], k2_ref[0], kc_ref[0]],
                          [v0_ref[0], v1_ref[0], v2_ref[0], vc_ref[0]], fixes, sink_ref, log2_units=True)
    o_ref[0] = out.astype(BF16)


def _swa(q, k, vt, kc, vtc, sink_tab):
    B, S, W = q.shape
    T = ATT_TILE
    nt = S // T
    tile = pl.BlockSpec((1, T, W), lambda b, i: (b, i, 0))
    ctx = lambda a: pl.BlockSpec((1,) + a.shape[1:], lambda b, i: (b, 0, 0))
    half = T // SWA_WINDOW
    prev = lambda i: jnp.maximum(half * i - 1, 0)
    nxt = lambda i: jnp.minimum(half * (i + 1), half * nt - 1)
    keys = [pl.BlockSpec((1, SWA_WINDOW, W), lambda b, i: (b, prev(i), 0)), tile,
            pl.BlockSpec((1, SWA_WINDOW, W), lambda b, i: (b, nxt(i), 0))]
    vals = [pl.BlockSpec((1, W, SWA_WINDOW), lambda b, i: (b, 0, prev(i))),
            pl.BlockSpec((1, W, T), lambda b, i: (b, 0, i)),
            pl.BlockSpec((1, W, SWA_WINDOW), lambda b, i: (b, 0, nxt(i)))]
    return pl.pallas_call(
        functools.partial(_swa_kernel, S),
        out_shape=jax.ShapeDtypeStruct((B, S, W), BF16),
        grid=(B, nt),
        in_specs=[tile] + keys + vals + [ctx(kc), ctx(vtc), _resident(sink_tab.shape)],
        out_specs=tile,
        compiler_params=_params(("arbitrary", "arbitrary")),
        name="windowed_attention",
    )(q, k, k, k, vt, vt, vt, kc, vtc, sink_tab)


def _mla_finish(acc, wuvt_ref, tq):
    o = (acc[0:MLA_KV_RANK] / acc[MLA_KV_RANK:MLA_KV_RANK + 1]).astype(BF16)
    outs = [_dot(wuvt_ref[hd], o[:, hd * tq:(hd + 1) * tq]) for hd in range(N_HEADS)]
    return jnp.concatenate(outs, axis=0).T


def _mla_kernel(n_chunks, unroll, tk, q_ref, k_ref, vt_ref, kc_ref, vtc_ref, wuvt_ref, o_ref,
                m_ref, ex_ref, acc_ref):
    tq = q_ref.shape[2]
    q = q_ref[0].reshape(N_HEADS * tq, q_ref.shape[3])

    def scores(c):
        off = pl.multiple_of(c * tk, tk)
        return _dot_nt(k_ref[0, pl.ds(off, tk), :], q)

    def values(c):
        return vt_ref[0, :, pl.ds(pl.multiple_of(c * tk, tk), tk)]

    def start():
        s = _dot_nt(kc_ref[0], q)
        m = jnp.max(s, axis=0, keepdims=True)
        m_ref[...] = m
        acc_ref[...] = _dot(vtc_ref[0], jnp.exp2(s - m).astype(BF16))

    def exact(c, carry):
        s = scores(c)
        m_old = m_ref[...]
        m_new = jnp.maximum(m_old, jnp.max(s, axis=0, keepdims=True))
        p = jnp.exp2(s - m_new).astype(BF16)
        m_ref[...] = m_new
        acc_ref[...] = jnp.exp2(m_old - m_new) * acc_ref[...] + _dot(values(c), p)
        return carry

    def group(cc, carry):
        m_old = m_ref[...]
        pv, cm = None, None
        for j in range(unroll):
            c = unroll * cc + j
            s = scores(c)
            p = jnp.exp2(s - m_old).astype(BF16)
            cj = jnp.max(s, axis=0, keepdims=True)
            cm = cj if cm is None else jnp.maximum(cm, cj)
            d = _dot(values(c), p)
            pv = d if pv is None else pv + d
        m_new = jnp.maximum(m_old, cm)
        ex_ref[...] = jnp.maximum(ex_ref[...], cm - m_old)
        m_ref[...] = m_new
        acc_ref[...] = (acc_ref[...] + pv) * jnp.exp2(m_old - m_new)
        return carry

    start()
    ex_ref[...] = jnp.zeros(ex_ref.shape, F32)
    lax.fori_loop(0, n_chunks // unroll, group, 0)

    @pl.when(jnp.max(ex_ref[...]) > MLA_LAG_LIMIT)
    def _():
        start()
        lax.fori_loop(0, n_chunks, exact, 0)

    o_ref[0] = _mla_finish(acc_ref[...], wuvt_ref, tq).astype(BF16)


def _mla(q, k, vt, kc, vtc, wuvt):
    B, H, S, W = q.shape
    tq, tk = MLA_TQ, MLA_TK
    n_chunks = S // tk
    assert S % tk == 0
    unroll = max(u for u in range(1, MLA_UNROLL + 1) if n_chunks % u == 0)
    per_batch = lambda shape: pl.BlockSpec((1,) + shape[1:], lambda b, i: (b,) + (0,) * (len(shape) - 1))
    return pl.pallas_call(
        functools.partial(_mla_kernel, n_chunks, unroll, tk),
        out_shape=jax.ShapeDtypeStruct((B, S, BRANCH_W), BF16),
        grid=(B, S // tq),
        in_specs=[pl.BlockSpec((1, H, tq, W), lambda b, i: (b, 0, i, 0)),
                  per_batch(k.shape), per_batch(vt.shape), per_batch(kc.shape), per_batch(vtc.shape),
                  _resident(wuvt.shape)],
        out_specs=pl.BlockSpec((1, tq, BRANCH_W), lambda b, i: (b, i, 0)),
        scratch_shapes=[pltpu.VMEM((1, H * tq), F32), pltpu.VMEM((1, H * tq), F32),
                        pltpu.VMEM((vt.shape[1], H * tq), F32)],
        compiler_params=_params(("arbitrary", "arbitrary")),
        name="latent_attention",
    )(q, k, vt, kc, vtc, wuvt)


def _ctx_attn_kernel(naq_ref, nak_ref, nav_ref, swq_ref, swk_ref, swv_ref, qmla_ref, kcat_ref, ckvt_ref,
                     sink_ref, wuvt_ref, na_o_ref, sw_o_ref, mla_o_ref):
    na_o_ref[0] = _attend_heads(naq_ref[0], [nak_ref[0]], [nav_ref[0]], [None], None).astype(BF16)
    sw_o_ref[0] = _attend_heads(swq_ref[0], [swk_ref[0]], [swv_ref[0]], [None], sink_ref,
                                log2_units=True).astype(BF16)
    tq = qmla_ref.shape[2]
    q = qmla_ref[0].reshape(N_HEADS * tq, qmla_ref.shape[3])
    s = _dot_nt(kcat_ref[0], q)
    p = jnp.exp2(s - jnp.max(s, axis=0, keepdims=True))
    acc = _dot(ckvt_ref[0], p.astype(BF16))
    mla_o_ref[0] = _mla_finish(acc, wuvt_ref, tq).astype(BF16)


def _ctx_attn(naq, nak, nav, swq, swk, swv, qmla, kcat, ckvt, sink_tab, wuvt):
    B, Lc, W = naq.shape
    per_batch = lambda a: pl.BlockSpec((1,) + a.shape[1:], lambda b: (b,) + (0,) * (a.ndim - 1))
    acts = [naq, nak, nav, swq, swk, swv, qmla, kcat, ckvt]
    out = jax.ShapeDtypeStruct((B, Lc, W), BF16)
    return pl.pallas_call(
        _ctx_attn_kernel,
        out_shape=[out, out, out],
        grid=(B,),
        in_specs=[per_batch(a) for a in acts] + [_resident(sink_tab.shape), _resident(wuvt.shape)],
        out_specs=[per_batch(naq)] * 3,
        compiler_params=_params(("arbitrary",)),
        name="context_attention",
    )(*acts, sink_tab, wuvt)


def _merge_kernel(x_ref, mod_ref, g_ref, y0_ref, y1_ref, y2_ref, y3_ref, wg_ref, wb_ref, wo_ref, o_ref):
    x = x_ref[0]
    D = x.shape[1]
    h = _modulated_norm(x, g_ref[...], mod_ref[0, 0:1, :], mod_ref[0, 1:2, :]).astype(BF16)
    merged = None
    for i, y_ref in enumerate((y0_ref, y1_ref, y2_ref, y3_ref)):
        gate = jax.nn.sigmoid(_dot(h, wg_ref[:, i * D:(i + 1) * D]))
        term = gate * _dot(y_ref[0], wb_ref[i])
        merged = term if merged is None else merged + term
    o_ref[0] = x + mod_ref[0, 2:3, :] * _dot(merged.astype(BF16), wo_ref[...])


def _mod_spec(mod, D):
    return pl.BlockSpec((1, 6, D), (lambda b, i: (b, 0, 0)) if mod.shape[0] > 1 else (lambda b, i: (0, 0, 0)))


def _merge(x, mod, g, ys, wg, wb, wo, tm):
    B, n, D = x.shape
    tok = lambda w: pl.BlockSpec((1, tm, w), lambda b, i: (b, i, 0))
    return pl.pallas_call(
        _merge_kernel,
        out_shape=jax.ShapeDtypeStruct((B, n, D), F32),
        grid=(B, n // tm),
        in_specs=[tok(D), _mod_spec(mod, D), _resident(g.shape)] + [tok(BRANCH_W)] * 4
                 + [_resident(wg.shape), _resident(wb.shape), _resident(wo.shape)],
        out_specs=tok(D),
        compiler_params=_params(("arbitrary", "arbitrary")),
        name="merge",
    )(x, mod, g, *ys, wg, wb, wo)


def _ffn_kernel(final, x_ref, mod_ref, g_ref, w1_ref, w3_ref, w2_ref, *rest):
    if final:
        fg_ref, o_ref = rest
    else:
        (o_ref,) = rest
    x = x_ref[0]
    h = _modulated_norm(x, g_ref[...], mod_ref[0, 3:4, :], mod_ref[0, 4:5, :]).astype(BF16)
    a = _dot(h, w1_ref[...])
    act = ((a * jax.nn.sigmoid(a)) * _dot(h, w3_ref[...])).astype(BF16)
    y = x + mod_ref[0, 5:6, :] * _dot(act, w2_ref[...])
    o_ref[0] = _rms(y, fg_ref[...]) if final else y


def _ffn(x, mod, g, w1, w3, w2, final_g, tm):
    B, n, D = x.shape
    final = final_g is not None
    tok = pl.BlockSpec((1, tm, D), lambda b, i: (b, i, 0))
    in_specs = [tok, _mod_spec(mod, D), _resident(g.shape), _resident(w1.shape), _resident(w3.shape),
                _resident(w2.shape)]
    args = [x, mod, g, w1, w3, w2]
    if final:
        in_specs.append(_resident(final_g.shape))
        args.append(final_g)
    return pl.pallas_call(
        functools.partial(_ffn_kernel, final),
        out_shape=jax.ShapeDtypeStruct((B, n, D), F32),
        grid=(B, n // tm),
        in_specs=in_specs, out_specs=tok,
        compiler_params=_params(("arbitrary", "arbitrary")),
        name="ffn_final" if final else "ffn",
    )(*args)


def _pack_w_in(w):
    def dup_heads(a):
        return jnp.concatenate([a[:, :64], a[:, :64], a[:, 64:], a[:, 64:]], axis=1)

    w = w.astype(BF16)
    kr = w[:, OFF_MLA_KR:KV_COLS]
    cols = [w[:, OFF_NA_Q:OFF_NA_Q + 256], w[:, OFF_NA_K:OFF_NA_K + 256], w[:, OFF_NA_V:OFF_NA_V + 256],
            w[:, OFF_SWA_Q:OFF_SWA_Q + 256], dup_heads(w[:, OFF_SWA_K:OFF_SWA_V]),
            dup_heads(w[:, OFF_SWA_V:OFF_MLA_CKV]), w[:, OFF_MLA_CKV:OFF_MLA_KR],
            jnp.concatenate([kr] * N_HEADS, axis=1), w[:, OFF_MLA_CQ:OFF_POOL], w[:, OFF_POOL:OFF_GATE]]
    return jnp.concatenate(cols, axis=1), w[:, OFF_GATE:]


def _block_diag(blocks):
    r, c = blocks[0].shape
    n = len(blocks)
    rows = [jnp.concatenate([blocks[i] if j == i else jnp.zeros((r, c), blocks[0].dtype) for j in range(n)], axis=1)
            for i in range(n)]
    return jnp.concatenate(rows, axis=0)


def _rope_tables(n, dim):
    rows = n // GRID_W
    n_freq = dim // 4
    inv = jnp.power(ROPE_BASE, -jnp.arange(n_freq, dtype=F32) / n_freq)
    reps = LANES // dim
    zeros = lambda a: jnp.zeros_like(a)

    def lanes(first, second, sign):
        head = jnp.concatenate([sign * first, sign * second, first, second], axis=1)
        return jnp.tile(head, (1, reps))

    out = []
    for count, is_row in ((rows, True), (GRID_W, False)):
        ang = jnp.arange(count, dtype=jnp.int32).astype(F32)[:, None] * inv
        cos, sin = jnp.cos(ang), jnp.sin(ang)
        if is_row:
            out += [lanes(cos, zeros(cos), 1.0), lanes(sin, zeros(sin), -1.0)]
        else:
            out += [lanes(zeros(cos), cos, 1.0), lanes(zeros(sin), sin, -1.0)]
    return tuple(out)


def _na_bias_tables(rpb, rows):
    T = ATT_TILE
    tr = T // GRID_W
    nt = rows // tr
    L, H = rpb.shape[:2]
    n_off = 2 * NA_ROWS - 1
    assert 3 * tr + tr - 1 == n_off
    pad = GRID_W - NA_COLS
    padded = jnp.pad(rpb.astype(F32), ((0, 0), (0, 0), (0, 0), (pad, pad)))
    toeplitz = jnp.stack([padded[..., GRID_W - 1 - qc:2 * GRID_W - 1 - qc] for qc in range(GRID_W)], axis=-2)
    qc, kc = np.arange(GRID_W)[:, None], np.arange(GRID_W)[None, :]
    c0 = np.clip(qc - NA_COLS // 2, 0, GRID_W - NA_COLS)
    col_ok = (kc >= c0) & (kc < c0 + NA_COLS)
    blocks = jnp.where(jnp.asarray(col_ok), toeplitz, NEG_INF)
    tabs = []
    for i in (0, 1, nt - 1):
        per_row = []
        for ql in range(tr):
            qr = tr * i + ql
            r0 = min(max(qr - NA_ROWS // 2, 0), rows - NA_ROWS)
            kr = tr * (i - 1) + np.arange(3 * tr)
            row_ok = (kr >= r0) & (kr < r0 + NA_ROWS)
            shifted = blocks[:, :, tr - 1 - ql:tr - 1 - ql + 3 * tr]
            per_row.append(jnp.where(jnp.asarray(row_ok)[:, None, None], shifted, NEG_INF))
        tab = jnp.stack(per_row, axis=2)
        tabs.append(jnp.transpose(tab, (0, 1, 2, 4, 3, 5)).reshape(L, H, T, 3 * T))
    return jnp.stack(tabs, axis=1)


def kernel(x, c, ctx, c_ctx, ada_w, ada_b, norm1_g, norm2_g, w_in, pool_w, pool_scale, na_rpb, swa_sink,
           mla_q_norm, mla_kv_norm, mla_w_uq, mla_w_uk, mla_w_uv, w_branch, w_out, ffn_w1, ffn_w3, ffn_w2,
           final_norm_g):
    B, S, D = x.shape
    Lc = ctx.shape[1]
    depth = ada_w.shape[0]
    rows = S // GRID_W
    assert S % 1024 == 0 and rows >= 3 * (ATT_TILE // GRID_W) and Lc == ATT_TILE and B <= 7
    tm_x, tm_c = 512, Lc

    cond = jnp.concatenate([c, c_ctx[None], jnp.zeros((8 - B - 1, D), F32)], axis=0)
    mods = _modulation(cond, ada_w, ada_b[:, None, :])
    rope_tabs = _rope_tables(S, HEAD_DIM) + _rope_tables(S, MLA_ROPE)
    bias_tabs = _na_bias_tables(na_rpb, rows)
    row2 = lambda v: v.reshape(1, -1)

    xc = ctx
    for l in range(depth):
        last = l == depth - 1
        mod_x = mods[l, :B].reshape(B, 6, D)
        mod_c = mods[l, B:B + 1].reshape(1, 6, D)
        wp, wg = _pack_w_in(w_in[l])
        uq = mla_w_uq[l].reshape(-1, N_HEADS, MLA_NOPE + MLA_ROPE)
        wuq = jnp.concatenate([uq[:, :, :MLA_NOPE].reshape(-1, N_HEADS * MLA_NOPE),
                               uq[:, :, MLA_NOPE:].reshape(-1, N_HEADS * MLA_ROPE)], axis=1).astype(BF16)
        uk = jnp.transpose(mla_w_uk[l], (1, 2, 0))
        wuk = _block_diag([uk[h] for h in range(N_HEADS)]).astype(BF16)
        wuvt = jnp.transpose(mla_w_uv[l], (1, 2, 0)).astype(BF16)
        w_pool = _block_diag([pool_w[l, g] for g in range(len(POOL_WINDOWS))]).astype(BF16)
        sink_tab = jnp.concatenate([jnp.broadcast_to(swa_sink[l][:, None], (N_HEADS, ATT_TILE)),
                                    jnp.zeros((8 - N_HEADS, ATT_TILE), F32)], axis=0)
        bias = bias_tabs[l]
        g1, g2 = row2(norm1_g[l]), row2(norm2_g[l])
        qg, kvg, psc = row2(mla_q_norm[l]), row2(mla_kv_norm[l]), row2(pool_scale[l])
        wb, wo = w_branch[l].astype(BF16), w_out[l].astype(BF16)
        w1, w3, w2 = ffn_w1[l].astype(BF16), ffn_w3[l].astype(BF16), ffn_w2[l].astype(BF16)

        (naq, nak, nav, swq, swk, _, kcat, ckvt, qmla, pool_in, swvt) = _inproj(
            x, mod_x, g1, wp, wuq, wuk, qg, kvg, rope_tabs, tm_x)
        (c_naq, c_nak, c_nav, c_swq, c_swk, c_swv, c_kcat, c_ckvt, c_qmla, c_pool_in, c_swvt) = _inproj(
            xc, mod_c, g1, wp, wuq, wuk, qg, kvg, None, tm_c)

        ys = [_pool(pool_in, w_pool, psc, 512),
              _na(naq, nak, nav, c_nak, c_nav, bias),
              _swa(swq, swk, swvt, c_swk, c_swvt, sink_tab),
              _mla(qmla, kcat, ckvt, c_kcat, c_ckvt, wuvt)]
        if not last:
            ys_c = [_pool(c_pool_in, w_pool, psc, Lc)] + list(_ctx_attn(
                c_naq, c_nak, c_nav, c_swq, c_swk, c_swv, c_qmla, c_kcat, c_ckvt, sink_tab, wuvt))
            xc = _merge(xc, mod_c, g1, ys_c, wg, wb, wo, tm_c)
            xc = _ffn(xc, mod_c, g2, w1, w3, w2, None, tm_c)
        x = _merge(x, mod_x, g1, ys, wg, wb, wo, tm_x)
        x = _ffn(x, mod_x, g2, w1, w3, w2, row2(final_norm_g) if last else None, tm_x)
    return x
```

```python
import functools

import numpy as np
import jax
import jax.numpy as jnp
from jax import lax
from jax.experimental import pallas as pl
from jax.experimental.pallas import tpu as pltpu

F32 = jnp.float32
BF16 = jnp.bfloat16

GRID_W = 64
HEAD_DIM = 64
ROPE_BASE = 10000.0
NEG_INF = -1e30
EPS = 1e-6
POOL_WINDOWS = (2, 4, 8, 16)
POOL_GROUP_DIM = 64
N_HEADS = 4
NA_ROWS = 8
NA_COLS = 16
SWA_WINDOW = 128
MLA_KV_RANK = 128
MLA_NOPE = 64
MLA_ROPE = 32
MLA_SCALE = (MLA_NOPE + MLA_ROPE) ** -0.5
LOG2_E = 1.4426950408889634
MLA_Q_SCALE = MLA_SCALE * LOG2_E
MLA_SUM_ROWS = 16
MLA_LAG_LIMIT = 64.0
ATT_SCALE = HEAD_DIM ** -0.5
BRANCH_W = 256

OFF_NA_K, OFF_NA_V, OFF_SWA_K, OFF_SWA_V, OFF_MLA_CKV, OFF_MLA_KR = 0, 256, 512, 640, 768, 896
KV_COLS = 928
OFF_NA_Q, OFF_SWA_Q, OFF_MLA_CQ, OFF_POOL, OFF_GATE = 928, 1184, 1440, 1696, 1952

LANES = 128
ATT_TILE = 256
MLA_TQ = 256
MLA_TK = 1024
MLA_UNROLL = 32
POOL_HALO = 8
VMEM_LIMIT = 56 * 1024 * 1024


def _params(sem):
    return pltpu.CompilerParams(dimension_semantics=sem, vmem_limit_bytes=VMEM_LIMIT)


def _dot(a, b):
    return jnp.dot(a, b, preferred_element_type=F32)


def _dot_nt(a, b):
    return lax.dot_general(a, b, (((1,), (1,)), ((), ())), preferred_element_type=F32)


def _resident(shape):
    n = len(shape)
    return pl.BlockSpec(shape, lambda *_: (0,) * n, pipeline_mode=pl.Buffered(1))


def _modulated_norm(x, g, shift, scale):
    y = x * lax.rsqrt(jnp.mean(x * x, axis=-1, keepdims=True) + EPS)
    return (y * g) * (1.0 + scale) + shift


def _rms(x, g):
    return x * lax.rsqrt(jnp.mean(x * x, axis=-1, keepdims=True) + EPS) * g


def _rope_lanes(t, cos, sin_signed, half):
    lane = lax.broadcasted_iota(jnp.int32, t.shape, 1)
    up = pltpu.roll(t, LANES - half, 1)
    down = pltpu.roll(t, half, 1)
    partner = jnp.where((lane & (2 * half - 1)) < half, up, down)
    return t * cos + partner * sin_signed


def _mod_kernel(c_ref, w_ref, b_ref, o_ref):
    c = c_ref[...]
    s = (c * jax.nn.sigmoid(c)).astype(BF16)
    o_ref[0] = _dot(s, w_ref[0].astype(BF16)) + b_ref[0]


def _modulation(cond, ada_w, ada_b):
    L, D, N = ada_w.shape
    tn = N // 4
    return pl.pallas_call(
        _mod_kernel,
        out_shape=jax.ShapeDtypeStruct((L, cond.shape[0], N), F32),
        grid=(L, N // tn),
        in_specs=[pl.BlockSpec(cond.shape, lambda l, j: (0, 0)),
                  pl.BlockSpec((1, D, tn), lambda l, j: (l, 0, j)),
                  pl.BlockSpec((1, 1, tn), lambda l, j: (l, 0, j))],
        out_specs=pl.BlockSpec((1, cond.shape[0], tn), lambda l, j: (l, 0, j)),
        compiler_params=_params(("arbitrary", "arbitrary")),
        name="modulation",
    )(cond, ada_w, ada_b)


def _inproj_kernel(rope, x_ref, mod_ref, g_ref, w_ref, wuq_ref, wuk_ref, qg_ref, kvg_ref, *rest):
    if rope:
        tm = x_ref.shape[1]
        g = tm // GRID_W

        def per_token(row_ref, col_ref):
            r = jnp.broadcast_to(row_ref[...][:, None, :], (g, GRID_W, LANES)).reshape(tm, LANES)
            c = jnp.broadcast_to(col_ref[...][None, :, :], (g, GRID_W, LANES)).reshape(tm, LANES)
            return r + c

        c64, s64 = per_token(rest[0], rest[2]), per_token(rest[1], rest[3])
        c32, s32 = per_token(rest[4], rest[6]), per_token(rest[5], rest[7])
        rest = rest[8:]
    (naq_ref, nak_ref, nav_ref, swq_ref, swk_ref, swv_ref, kcat_ref, ckvt_ref, qmla_ref, pool_ref,
     swvt_ref) = rest

    x = x_ref[0]
    h = _modulated_norm(x, g_ref[...], mod_ref[0, 0:1, :], mod_ref[0, 1:2, :]).astype(BF16)

    def seg(a, n):
        return _dot(h, w_ref[:, a:a + n])

    def rope64(t):
        if not rope:
            return t
        return jnp.concatenate([_rope_lanes(t[:, :LANES], c64, s64, 32),
                                _rope_lanes(t[:, LANES:], c64, s64, 32)], axis=1)

    def rope32(t):
        return _rope_lanes(t, c32, s32, 16) if rope else t

    naq_ref[0] = (seg(0, 256) * ATT_SCALE).astype(BF16)
    nak_ref[0] = seg(256, 256).astype(BF16)
    nav_ref[0] = seg(512, 256).astype(BF16)
    swq_ref[0] = (rope64(seg(768, 256)) * (ATT_SCALE * LOG2_E)).astype(BF16)
    swk_ref[0] = rope64(seg(1024, 256)).astype(BF16)
    swv = seg(1280, 256)
    swv_ref[0] = swv.astype(BF16)
    swvt_ref[0] = swv.T.astype(BF16)

    ckv = _rms(seg(1536, 128), kvg_ref[...])
    kcat_ref[0, :, 0:LANES] = ckv.astype(BF16)
    kcat_ref[0, :, LANES:2 * LANES] = rope32(seg(1664, 128)).astype(BF16)
    ckvt_ref[0, 0:MLA_KV_RANK, :] = ckv.T.astype(BF16)
    ckvt_ref[0, MLA_KV_RANK:, :] = jnp.ones((MLA_SUM_ROWS, ckv.shape[0]), BF16)

    cq = _rms(seg(1792, 256), qg_ref[...]).astype(BF16)
    qq = _dot(cq, wuq_ref[...])
    q_lat = _dot(qq[:, :256].astype(BF16), wuk_ref[...])
    q_rope = rope32(qq[:, 256:384])
    lane = lax.broadcasted_iota(jnp.int32, q_rope.shape, 1)
    for hd in range(N_HEADS):
        qmla_ref[0, hd, :, 0:LANES] = (q_lat[:, hd * LANES:(hd + 1) * LANES] * MLA_Q_SCALE).astype(BF16)
        own = (lane >= hd * MLA_ROPE) & (lane < (hd + 1) * MLA_ROPE)
        qmla_ref[0, hd, :, LANES:2 * LANES] = (jnp.where(own, q_rope, 0.0) * MLA_Q_SCALE).astype(BF16)

    pool_ref[0] = seg(2048, 256)


def _inproj(x, mod, g, wp, wuq, wuk, qg, kvg, rope_tabs, tm):
    B, n, D = x.shape
    rope = rope_tabs is not None
    mod_b = mod.shape[0]
    tok = lambda w: pl.BlockSpec((1, tm, w), lambda b, i: (b, i, 0))
    in_specs = [tok(D),
                pl.BlockSpec((1, 6, D), (lambda b, i: (b, 0, 0)) if mod_b > 1 else (lambda b, i: (0, 0, 0))),
                _resident(g.shape), _resident(wp.shape), _resident(wuq.shape), _resident(wuk.shape),
                _resident(qg.shape), _resident(kvg.shape)]
    args = [x, mod, g, wp, wuq, wuk, qg, kvg]
    if rope:
        assert tm % (8 * GRID_W) == 0
        row_tab = pl.BlockSpec((tm // GRID_W, LANES), lambda b, i: (i, 0))
        col_tab = _resident((GRID_W, LANES))
        in_specs += [row_tab, row_tab, col_tab, col_tab] * 2
        args += list(rope_tabs)
    bf = lambda w: jax.ShapeDtypeStruct((B, n, w), BF16)
    vt_rows = MLA_KV_RANK + MLA_SUM_ROWS
    out_shape = [bf(256)] * 7 + [jax.ShapeDtypeStruct((B, vt_rows, n), BF16),
                                 jax.ShapeDtypeStruct((B, N_HEADS, n, 256), BF16),
                                 jax.ShapeDtypeStruct((B, n, 256), F32), jax.ShapeDtypeStruct((B, 256, n), BF16)]
    out_specs = [tok(256)] * 7 + [pl.BlockSpec((1, vt_rows, tm), lambda b, i: (b, 0, i)),
                                  pl.BlockSpec((1, N_HEADS, tm, 256), lambda b, i: (b, 0, i, 0)),
                                  tok(256), pl.BlockSpec((1, 256, tm), lambda b, i: (b, 0, i))]
    return pl.pallas_call(
        functools.partial(_inproj_kernel, rope),
        out_shape=out_shape, grid=(B, n // tm), in_specs=in_specs, out_specs=out_specs,
        compiler_params=_params(("arbitrary", "arbitrary")),
        name="inproj_rope" if rope else "inproj_ctx",
    )(*args)


def _pool_kernel(n, tp, u_ref, prev_ref, next_ref, w_ref, sc_ref, o_ref, ext_ref):
    i = pl.program_id(1)
    last = pl.num_programs(1) - 1
    ext_ref[0:POOL_HALO, :] = jnp.where(i > 0, prev_ref[0], 0.0)
    ext_ref[POOL_HALO:POOL_HALO + tp, :] = u_ref[0]
    ext_ref[POOL_HALO + tp:, :] = jnp.where(i < last, next_ref[0], 0.0)

    t = i * tp + lax.broadcasted_iota(jnp.int32, (tp, LANES), 0)
    lane = lax.broadcasted_iota(jnp.int32, (tp, LANES), 1)

    def window_mean(col, w):
        acc = None
        for k in range(-(w // 2), w - w // 2):
            v = ext_ref[POOL_HALO + k:POOL_HALO + k + tp, col * LANES:(col + 1) * LANES]
            acc = v if acc is None else acc + v
        lo = jnp.clip(t - w // 2, 0, n)
        hi = jnp.clip(t - w // 2 + w, 0, n)
        return acc / (hi - lo).astype(F32)

    first = lane < POOL_GROUP_DIM
    m01 = jnp.where(first, window_mean(0, POOL_WINDOWS[0]), window_mean(0, POOL_WINDOWS[1]))
    m23 = jnp.where(first, window_mean(1, POOL_WINDOWS[2]), window_mean(1, POOL_WINDOWS[3]))
    y = (jnp.concatenate([m01, m23], axis=1) - u_ref[0]).astype(BF16)
    o_ref[0] = (_dot(y, w_ref[...]) * sc_ref[...]).astype(BF16)


def _pool(u, w_bd, scale, tp):
    B, n, W = u.shape
    hb = tp // POOL_HALO
    nh = n // POOL_HALO
    return pl.pallas_call(
        functools.partial(_pool_kernel, n, tp),
        out_shape=jax.ShapeDtypeStruct((B, n, W), BF16),
        grid=(B, n // tp),
        in_specs=[pl.BlockSpec((1, tp, W), lambda b, i: (b, i, 0)),
                  pl.BlockSpec((1, POOL_HALO, W), lambda b, i: (b, jnp.maximum(i * hb - 1, 0), 0)),
                  pl.BlockSpec((1, POOL_HALO, W), lambda b, i: (b, jnp.minimum((i + 1) * hb, nh - 1), 0)),
                  _resident(w_bd.shape), _resident(scale.shape)],
        out_specs=pl.BlockSpec((1, tp, W), lambda b, i: (b, i, 0)),
        scratch_shapes=[pltpu.VMEM((tp + 2 * POOL_HALO, W), F32)],
        compiler_params=_params(("arbitrary", "arbitrary")),
        name="pool",
    )(u, u, u, w_bd, scale)


def _head_mask(shape, hd):
    lane = lax.broadcasted_iota(jnp.int32, shape, 1)
    return (lane >= hd * HEAD_DIM) & (lane < (hd + 1) * HEAD_DIM)


def _exp_fn(log2_units):
    return jnp.exp2 if log2_units else jnp.exp


def _attend_heads(q, key_parts, val_parts, fix_parts, sink_ref, log2_units=False):
    exp = _exp_fn(log2_units)
    out = jnp.zeros(q.shape, F32)
    for hd in range(N_HEADS):
        qh = jnp.where(_head_mask(q.shape, hd), q, jnp.zeros_like(q))
        scores = []
        for kp, fix in zip(key_parts, fix_parts):
            s = _dot_nt(qh, kp)
            scores.append(s if fix is None else fix(hd, s))
        m = functools.reduce(jnp.maximum, [jnp.max(s, axis=-1, keepdims=True) for s in scores])
        if sink_ref is not None:
            sink = sink_ref[hd:hd + 1, 0:1] * (LOG2_E if log2_units else 1.0)
            m = jnp.maximum(m, sink)
        probs = [exp(s - m) for s in scores]
        den = functools.reduce(jnp.add, [jnp.sum(p, axis=-1, keepdims=True) for p in probs])
        if sink_ref is not None:
            den = den + exp(sink - m)
        o = functools.reduce(jnp.add, [_dot(p.astype(BF16), vp) for p, vp in zip(probs, val_parts)])
        out = jnp.where(_head_mask(out.shape, hd), o / den, out)
    return out


def _attend_heads_t(q, key_parts, valt_parts, fix_parts, sink_ref, log2_units=False):
    exp = _exp_fn(log2_units)
    outs = []
    for hd in range(N_HEADS):
        qh = jnp.where(_head_mask(q.shape, hd), q, jnp.zeros_like(q))
        scores = []
        for kp, fix in zip(key_parts, fix_parts):
            s = _dot_nt(kp, qh)
            scores.append(s if fix is None else fix(hd, s))
        m = functools.reduce(jnp.maximum, [jnp.max(s, axis=0, keepdims=True) for s in scores])
        if sink_ref is not None:
            sink = sink_ref[hd:hd + 1, :] * (LOG2_E if log2_units else 1.0)
            m = jnp.maximum(m, sink)
        probs = [exp(s - m) for s in scores]
        den = functools.reduce(jnp.add, [jnp.sum(p, axis=0, keepdims=True) for p in probs])
        if sink_ref is not None:
            den = den + exp(sink - m)
        o = functools.reduce(jnp.add, [_dot(vt[hd * HEAD_DIM:(hd + 1) * HEAD_DIM, :], p.astype(BF16))
                                       for p, vt in zip(probs, valt_parts)])
        outs.append(o / den)
    return jnp.concatenate(outs, axis=0).T


def _na_kernel(q_ref, k0_ref, k1_ref, k2_ref, v0_ref, v1_ref, v2_ref, kc_ref, vc_ref, bias_ref, o_ref):
    T = ATT_TILE
    fixes = [(lambda hd, s, j=j: s + bias_ref[0, hd, :, j * T:(j + 1) * T]) for j in range(3)] + [None]
    out = _attend_heads(q_ref[0], [k0_ref[0], k1_ref[0], k2_ref[0], kc_ref[0]],
                        [v0_ref[0], v1_ref[0], v2_ref[0], vc_ref[0]], fixes, None)
    o_ref[0] = out.astype(BF16)


def _neighbour_specs(nt):
    T = ATT_TILE
    lo, mid, hi = (lambda i: jnp.maximum(i - 1, 0)), (lambda i: i), (lambda i: jnp.minimum(i + 1, nt - 1))
    return [pl.BlockSpec((1, T, 256), lambda b, i, f=f: (b, f(i), 0)) for f in (lo, mid, hi)]


def _na(q, k, v, kc, vc, bias):
    B, S, W = q.shape
    T = ATT_TILE
    nt = S // T
    tile = pl.BlockSpec((1, T, W), lambda b, i: (b, i, 0))
    ctx = lambda a: pl.BlockSpec((1,) + a.shape[1:], lambda b, i: (b, 0, 0))
    variant = lambda b, i: (jnp.where(i == 0, 0, jnp.where(i == nt - 1, 2, 1)), 0, 0, 0)
    return pl.pallas_call(
        _na_kernel,
        out_shape=jax.ShapeDtypeStruct((B, S, W), BF16),
        grid=(B, nt),
        in_specs=[tile] + _neighbour_specs(nt) + _neighbour_specs(nt) + [ctx(kc), ctx(vc),
                  pl.BlockSpec((1,) + bias.shape[1:], variant)],
        out_specs=tile,
        compiler_params=_params(("arbitrary", "arbitrary")),
        name="neighbourhood_attention",
    )(q, k, k, k, v, v, v, kc, vc, bias)


def _swa_kernel(S, q_ref, k0_ref, k1_ref, k2_ref, v0_ref, v1_ref, v2_ref, kc_ref, vc_ref, sink_ref, o_ref):
    T = ATT_TILE
    i = pl.program_id(1)

    def band(first_key, n_keys):
        kpos = first_key + lax.broadcasted_iota(jnp.int32, (n_keys, T), 0)
        qpos = i * T + lax.broadcasted_iota(jnp.int32, (n_keys, T), 1)
        ok = (jnp.abs(kpos - qpos) <= SWA_WINDOW) & (kpos >= 0) & (kpos < S)
        return lambda hd, s: jnp.where(ok, s, NEG_INF)

    fixes = [band(i * T - SWA_WINDOW, SWA_WINDOW), band(i * T, T), band((i + 1) * T, SWA_WINDOW), None]
    out = _attend_heads_t(q_ref[0], [k0_ref[0], k1_ref[0], k2_ref[0], kc_ref[0]],
                          [v0_ref[0], v1_ref[0], v2_ref[0], vc_ref[0]], fixes, sink_ref, log2_units=True)
    o_ref[0] = out.astype(BF16)


def _swa(q, k, vt, kc, vtc, sink_tab):
    B, S, W = q.shape
    T = ATT_TILE
    nt = S // T
    tile = pl.BlockSpec((1, T, W), lambda b, i: (b, i, 0))
    ctx = lambda a: pl.BlockSpec((1,) + a.shape[1:], lambda b, i: (b, 0, 0))
    half = T // SWA_WINDOW
    prev = lambda i: jnp.maximum(half * i - 1, 0)
    nxt = lambda i: jnp.minimum(half * (i + 1), half * nt - 1)
    keys = [pl.BlockSpec((1, SWA_WINDOW, W), lambda b, i: (b, prev(i), 0)), tile,
            pl.BlockSpec((1, SWA_WINDOW, W), lambda b, i: (b, nxt(i), 0))]
    vals = [pl.BlockSpec((1, W, SWA_WINDOW), lambda b, i: (b, 0, prev(i))),
            pl.BlockSpec((1, W, T), lambda b, i: (b, 0, i)),
            pl.BlockSpec((1, W, SWA_WINDOW), lambda b, i: (b, 0, nxt(i)))]
    return pl.pallas_call(
        functools.partial(_swa_kernel, S),
        out_shape=jax.ShapeDtypeStruct((B, S, W), BF16),
        grid=(B, nt),
        in_specs=[tile] + keys + vals + [ctx(kc), ctx(vtc), _resident(sink_tab.shape)],
        out_specs=tile,
        compiler_params=_params(("arbitrary", "arbitrary")),
        name="windowed_attention",
    )(q, k, k, k, vt, vt, vt, kc, vtc, sink_tab)


def _mla_finish(acc, wuvt_ref, tq):
    o = (acc[0:MLA_KV_RANK] / acc[MLA_KV_RANK:MLA_KV_RANK + 1]).astype(BF16)
    outs = [_dot(wuvt_ref[hd], o[:, hd * tq:(hd + 1) * tq]) for hd in range(N_HEADS)]
    return jnp.concatenate(outs, axis=0).T


def _mla_kernel(n_chunks, unroll, tk, q_ref, k_ref, vt_ref, kc_ref, vtc_ref, wuvt_ref, o_ref,
                m_ref, ex_ref, acc_ref):
    tq = q_ref.shape[2]
    q = q_ref[0].reshape(N_HEADS * tq, q_ref.shape[3])

    def scores(c):
        off = pl.multiple_of(c * tk, tk)
        return _dot_nt(k_ref[0, pl.ds(off, tk), :], q)

    def values(c):
        return vt_ref[0, :, pl.ds(pl.multiple_of(c * tk, tk), tk)]

    def start():
        s = _dot_nt(kc_ref[0], q)
        m = jnp.max(s, axis=0, keepdims=True)
        m_ref[...] = m
        acc_ref[...] = _dot(vtc_ref[0], jnp.exp2(s - m).astype(BF16))

    def exact(c, carry):
        s = scores(c)
        m_old = m_ref[...]
        m_new = jnp.maximum(m_old, jnp.max(s, axis=0, keepdims=True))
        p = jnp.exp2(s - m_new).astype(BF16)
        m_ref[...] = m_new
        acc_ref[...] = jnp.exp2(m_old - m_new) * acc_ref[...] + _dot(values(c), p)
        return carry

    def group(cc, carry):
        m_old = m_ref[...]
        pv, cm = None, None
        for j in range(unroll):
            c = unroll * cc + j
            s = scores(c)
            p = jnp.exp2(s - m_old).astype(BF16)
            cj = jnp.max(s, axis=0, keepdims=True)
            cm = cj if cm is None else jnp.maximum(cm, cj)
            d = _dot(values(c), p)
            pv = d if pv is None else pv + d
        m_new = jnp.maximum(m_old, cm)
        ex_ref[...] = jnp.maximum(ex_ref[...], cm - m_old)
        m_ref[...] = m_new
        acc_ref[...] = (acc_ref[...] + pv) * jnp.exp2(m_old - m_new)
        return carry

    start()
    ex_ref[...] = jnp.zeros(ex_ref.shape, F32)
    lax.fori_loop(0, n_chunks // unroll, group, 0)

    @pl.when(jnp.max(ex_ref[...]) > MLA_LAG_LIMIT)
    def _():
        start()
        lax.fori_loop(0, n_chunks, exact, 0)

    o_ref[0] = _mla_finish(acc_ref[...], wuvt_ref, tq).astype(BF16)


def _mla(q, k, vt, kc, vtc, wuvt):
    B, H, S, W = q.shape
    tq, tk = MLA_TQ, MLA_TK
    n_chunks = S // tk
    assert S % tk == 0
    unroll = max(u for u in range(1, MLA_UNROLL + 1) if n_chunks % u == 0)
    per_batch = lambda shape: pl.BlockSpec((1,) + shape[1:], lambda b, i: (b,) + (0,) * (len(shape) - 1))
    return pl.pallas_call(
        functools.partial(_mla_kernel, n_chunks, unroll, tk),
        out_shape=jax.ShapeDtypeStruct((B, S, BRANCH_W), BF16),
        grid=(B, S // tq),
        in_specs=[pl.BlockSpec((1, H, tq, W), lambda b, i: (b, 0, i, 0)),
                  per_batch(k.shape), per_batch(vt.shape), per_batch(kc.shape), per_batch(vtc.shape),
                  _resident(wuvt.shape)],
        out_specs=pl.BlockSpec((1, tq, BRANCH_W), lambda b, i: (b, i, 0)),
        scratch_shapes=[pltpu.VMEM((1, H * tq), F32), pltpu.VMEM((1, H * tq), F32),
                        pltpu.VMEM((vt.shape[1], H * tq), F32)],
        compiler_params=_params(("arbitrary", "arbitrary")),
        name="latent_attention",
    )(q, k, vt, kc, vtc, wuvt)


def _ctx_attn_kernel(naq_ref, nak_ref, nav_ref, swq_ref, swk_ref, swv_ref, qmla_ref, kcat_ref, ckvt_ref,
                     sink_ref, wuvt_ref, na_o_ref, sw_o_ref, mla_o_ref):
    na_o_ref[0] = _attend_heads(naq_ref[0], [nak_ref[0]], [nav_ref[0]], [None], None).astype(BF16)
    sw_o_ref[0] = _attend_heads(swq_ref[0], [swk_ref[0]], [swv_ref[0]], [None], sink_ref,
                                log2_units=True).astype(BF16)
    tq = qmla_ref.shape[2]
    q = qmla_ref[0].reshape(N_HEADS * tq, qmla_ref.shape[3])
    s = _dot_nt(kcat_ref[0], q)
    p = jnp.exp2(s - jnp.max(s, axis=0, keepdims=True))
    acc = _dot(ckvt_ref[0], p.astype(BF16))
    mla_o_ref[0] = _mla_finish(acc, wuvt_ref, tq).astype(BF16)


def _ctx_attn(naq, nak, nav, swq, swk, swv, qmla, kcat, ckvt, sink_tab, wuvt):
    B, Lc, W = naq.shape
    per_batch = lambda a: pl.BlockSpec((1,) + a.shape[1:], lambda b: (b,) + (0,) * (a.ndim - 1))
    acts = [naq, nak, nav, swq, swk, swv, qmla, kcat, ckvt]
    out = jax.ShapeDtypeStruct((B, Lc, W), BF16)
    return pl.pallas_call(
        _ctx_attn_kernel,
        out_shape=[out, out, out],
        grid=(B,),
        in_specs=[per_batch(a) for a in acts] + [_resident(sink_tab.shape), _resident(wuvt.shape)],
        out_specs=[per_batch(naq)] * 3,
        compiler_params=_params(("arbitrary",)),
        name="context_attention",
    )(*acts, sink_tab, wuvt)


def _merge_kernel(x_ref, mod_ref, g_ref, y0_ref, y1_ref, y2_ref, y3_ref, wg_ref, wb_ref, wo_ref, o_ref):
    x = x_ref[0]
    D = x.shape[1]
    h = _modulated_norm(x, g_ref[...], mod_ref[0, 0:1, :], mod_ref[0, 1:2, :]).astype(BF16)
    merged = None
    for i, y_ref in enumerate((y0_ref, y1_ref, y2_ref, y3_ref)):
        gate = jax.nn.sigmoid(_dot(h, wg_ref[:, i * D:(i + 1) * D]))
        term = gate * _dot(y_ref[0], wb_ref[i])
        merged = term if merged is None else merged + term
    o_ref[0] = x + mod_ref[0, 2:3, :] * _dot(merged.astype(BF16), wo_ref[...])


def _mod_spec(mod, D):
    return pl.BlockSpec((1, 6, D), (lambda b, i: (b, 0, 0)) if mod.shape[0] > 1 else (lambda b, i: (0, 0, 0)))


def _merge(x, mod, g, ys, wg, wb, wo, tm):
    B, n, D = x.shape
    tok = lambda w: pl.BlockSpec((1, tm, w), lambda b, i: (b, i, 0))
    return pl.pallas_call(
        _merge_kernel,
        out_shape=jax.ShapeDtypeStruct((B, n, D), F32),
        grid=(B, n // tm),
        in_specs=[tok(D), _mod_spec(mod, D), _resident(g.shape)] + [tok(BRANCH_W)] * 4
                 + [_resident(wg.shape), _resident(wb.shape), _resident(wo.shape)],
        out_specs=tok(D),
        compiler_params=_params(("arbitrary", "arbitrary")),
        name="merge",
    )(x, mod, g, *ys, wg, wb, wo)


def _ffn_kernel(final, x_ref, mod_ref, g_ref, w1_ref, w3_ref, w2_ref, *rest):
    if final:
        fg_ref, o_ref = rest
    else:
        (o_ref,) = rest
    x = x_ref[0]
    h = _modulated_norm(x, g_ref[...], mod_ref[0, 3:4, :], mod_ref[0, 4:5, :]).astype(BF16)
    a = _dot(h, w1_ref[...])
    act = ((a * jax.nn.sigmoid(a)) * _dot(h, w3_ref[...])).astype(BF16)
    y = x + mod_ref[0, 5:6, :] * _dot(act, w2_ref[...])
    o_ref[0] = _rms(y, fg_ref[...]) if final else y


def _ffn(x, mod, g, w1, w3, w2, final_g, tm):
    B, n, D = x.shape
    final = final_g is not None
    tok = pl.BlockSpec((1, tm, D), lambda b, i: (b, i, 0))
    in_specs = [tok, _mod_spec(mod, D), _resident(g.shape), _resident(w1.shape), _resident(w3.shape),
                _resident(w2.shape)]
    args = [x, mod, g, w1, w3, w2]
    if final:
        in_specs.append(_resident(final_g.shape))
        args.append(final_g)
    return pl.pallas_call(
        functools.partial(_ffn_kernel, final),
        out_shape=jax.ShapeDtypeStruct((B, n, D), F32),
        grid=(B, n // tm),
        in_specs=in_specs, out_specs=tok,
        compiler_params=_params(("arbitrary", "arbitrary")),
        name="ffn_final" if final else "ffn",
    )(*args)


def _pack_w_in(w):
    def dup_heads(a):
        return jnp.concatenate([a[:, :64], a[:, :64], a[:, 64:], a[:, 64:]], axis=1)

    w = w.astype(BF16)
    kr = w[:, OFF_MLA_KR:KV_COLS]
    cols = [w[:, OFF_NA_Q:OFF_NA_Q + 256], w[:, OFF_NA_K:OFF_NA_K + 256], w[:, OFF_NA_V:OFF_NA_V + 256],
            w[:, OFF_SWA_Q:OFF_SWA_Q + 256], dup_heads(w[:, OFF_SWA_K:OFF_SWA_V]),
            dup_heads(w[:, OFF_SWA_V:OFF_MLA_CKV]), w[:, OFF_MLA_CKV:OFF_MLA_KR],
            jnp.concatenate([kr] * N_HEADS, axis=1), w[:, OFF_MLA_CQ:OFF_POOL], w[:, OFF_POOL:OFF_GATE]]
    return jnp.concatenate(cols, axis=1), w[:, OFF_GATE:]


def _block_diag(blocks):
    r, c = blocks[0].shape
    n = len(blocks)
    rows = [jnp.concatenate([blocks[i] if j == i else jnp.zeros((r, c), blocks[0].dtype) for j in range(n)], axis=1)
            for i in range(n)]
    return jnp.concatenate(rows, axis=0)


def _rope_tables(n, dim):
    rows = n // GRID_W
    n_freq = dim // 4
    inv = jnp.power(ROPE_BASE, -jnp.arange(n_freq, dtype=F32) / n_freq)
    reps = LANES // dim
    zeros = lambda a: jnp.zeros_like(a)

    def lanes(first, second, sign):
        head = jnp.concatenate([sign * first, sign * second, first, second], axis=1)
        return jnp.tile(head, (1, reps))

    out = []
    for count, is_row in ((rows, True), (GRID_W, False)):
        ang = jnp.arange(count, dtype=jnp.int32).astype(F32)[:, None] * inv
        cos, sin = jnp.cos(ang), jnp.sin(ang)
        if is_row:
            out += [lanes(cos, zeros(cos), 1.0), lanes(sin, zeros(sin), -1.0)]
        else:
            out += [lanes(zeros(cos), cos, 1.0), lanes(zeros(sin), sin, -1.0)]
    return tuple(out)


def _na_bias_tables(rpb, rows):
    T = ATT_TILE
    tr = T // GRID_W
    nt = rows // tr
    L, H = rpb.shape[:2]
    n_off = 2 * NA_ROWS - 1
    assert 3 * tr + tr - 1 == n_off
    pad = GRID_W - NA_COLS
    padded = jnp.pad(rpb.astype(F32), ((0, 0), (0, 0), (0, 0), (pad, pad)))
    toeplitz = jnp.stack([padded[..., GRID_W - 1 - qc:2 * GRID_W - 1 - qc] for qc in range(GRID_W)], axis=-2)
    qc, kc = np.arange(GRID_W)[:, None], np.arange(GRID_W)[None, :]
    c0 = np.clip(qc - NA_COLS // 2, 0, GRID_W - NA_COLS)
    col_ok = (kc >= c0) & (kc < c0 + NA_COLS)
    blocks = jnp.where(jnp.asarray(col_ok), toeplitz, NEG_INF)
    tabs = []
    for i in (0, 1, nt - 1):
        per_row = []
        for ql in range(tr):
            qr = tr * i + ql
            r0 = min(max(qr - NA_ROWS // 2, 0), rows - NA_ROWS)
            kr = tr * (i - 1) + np.arange(3 * tr)
            row_ok = (kr >= r0) & (kr < r0 + NA_ROWS)
            shifted = blocks[:, :, tr - 1 - ql:tr - 1 - ql + 3 * tr]
            per_row.append(jnp.where(jnp.asarray(row_ok)[:, None, None], shifted, NEG_INF))
        tab = jnp.stack(per_row, axis=2)
        tabs.append(jnp.transpose(tab, (0, 1, 2, 4, 3, 5)).reshape(L, H, T, 3 * T))
    return jnp.stack(tabs, axis=1)


def kernel(x, c, ctx, c_ctx, ada_w, ada_b, norm1_g, norm2_g, w_in, pool_w, pool_scale, na_rpb, swa_sink,
           mla_q_norm, mla_kv_norm, mla_w_uq, mla_w_uk, mla_w_uv, w_branch, w_out, ffn_w1, ffn_w3, ffn_w2,
           final_norm_g):
    B, S, D = x.shape
    Lc = ctx.shape[1]
    depth = ada_w.shape[0]
    rows = S // GRID_W
    assert S % 1024 == 0 and rows >= 3 * (ATT_TILE // GRID_W) and Lc == ATT_TILE and B <= 7
    tm_x, tm_c = 512, Lc

    cond = jnp.concatenate([c, c_ctx[None], jnp.zeros((8 - B - 1, D), F32)], axis=0)
    mods = _modulation(cond, ada_w, ada_b[:, None, :])
    rope_tabs = _rope_tables(S, HEAD_DIM) + _rope_tables(S, MLA_ROPE)
    bias_tabs = _na_bias_tables(na_rpb, rows)
    row2 = lambda v: v.reshape(1, -1)

    xc = ctx
    for l in range(depth):
        last = l == depth - 1
        mod_x = mods[l, :B].reshape(B, 6, D)
        mod_c = mods[l, B:B + 1].reshape(1, 6, D)
        wp, wg = _pack_w_in(w_in[l])
        uq = mla_w_uq[l].reshape(-1, N_HEADS, MLA_NOPE + MLA_ROPE)
        wuq = jnp.concatenate([uq[:, :, :MLA_NOPE].reshape(-1, N_HEADS * MLA_NOPE),
                               uq[:, :, MLA_NOPE:].reshape(-1, N_HEADS * MLA_ROPE)], axis=1).astype(BF16)
        uk = jnp.transpose(mla_w_uk[l], (1, 2, 0))
        wuk = _block_diag([uk[h] for h in range(N_HEADS)]).astype(BF16)
        wuvt = jnp.transpose(mla_w_uv[l], (1, 2, 0)).astype(BF16)
        w_pool = _block_diag([pool_w[l, g] for g in range(len(POOL_WINDOWS))]).astype(BF16)
        sink_tab = jnp.concatenate([jnp.broadcast_to(swa_sink[l][:, None], (N_HEADS, ATT_TILE)),
                                    jnp.zeros((8 - N_HEADS, ATT_TILE), F32)], axis=0)
        bias = bias_tabs[l]
        g1, g2 = row2(norm1_g[l]), row2(norm2_g[l])
        qg, kvg, psc = row2(mla_q_norm[l]), row2(mla_kv_norm[l]), row2(pool_scale[l])
        wb, wo = w_branch[l].astype(BF16), w_out[l].astype(BF16)
        w1, w3, w2 = ffn_w1[l].astype(BF16), ffn_w3[l].astype(BF16), ffn_w2[l].astype(BF16)

        (naq, nak, nav, swq, swk, _, kcat, ckvt, qmla, pool_in, swvt) = _inproj(
            x, mod_x, g1, wp, wuq, wuk, qg, kvg, rope_tabs, tm_x)
        (c_naq, c_nak, c_nav, c_swq, c_swk, c_swv, c_kcat, c_ckvt, c_qmla, c_pool_in, c_swvt) = _inproj(
            xc, mod_c, g1, wp, wuq, wuk, qg, kvg, None, tm_c)

        ys = [_pool(pool_in, w_pool, psc, 512),
              _na(naq, nak, nav, c_nak, c_nav, bias),
              _swa(swq, swk, swvt, c_swk, c_swvt, sink_tab),
              _mla(qmla, kcat, ckvt, c_kcat, c_ckvt, wuvt)]
        if not last:
            ys_c = [_pool(c_pool_in, w_pool, psc, Lc)] + list(_ctx_attn(
                c_naq, c_nak, c_nav, c_swq, c_swk, c_swv, c_qmla, c_kcat, c_ckvt, sink_tab, wuvt))
            xc = _merge(xc, mod_c, g1, ys_c, wg, wb, wo, tm_c)
            xc = _ffn(xc, mod_c, g2, w1, w3, w2, None, tm_c)
        x = _merge(x, mod_x, g1, ys, wg, wb, wo, tm_x)
        x = _ffn(x, mod_x, g2, w1, w3, w2, row2(final_norm_g) if last else None, tm_x)
    return x
```
